```python
import math
import jax, jax.numpy as jnp
from jax import lax
import numpy as np

D_MODEL = 1024
BATCH = 16
SEQ = 2048
DEPTH = 2

N_MIXERS = 2
NORM_EPS = 1e-6

GLA_HEADS = 4
GLA_DK = D_MODEL // 2
GLA_DV = D_MODEL
GLA_DK_HEAD = GLA_DK // GLA_HEADS
GLA_DV_HEAD = GLA_DV // GLA_HEADS
GLA_RANK = 16
GLA_TAU = 16.0
GLA_CHUNK = 64
GLA_IN = 2 * GLA_DK + GLA_DV + GLA_RANK + GLA_DV

DSW_HEADS = 16
DSW_HEAD_DIM = D_MODEL // DSW_HEADS
DSW_WIDTH = DSW_HEADS * DSW_HEAD_DIM
DSW_GROUPS = ((128, 1), (512, 4), (2048, 16))
N_GROUPS = len(DSW_GROUPS)
DSW_IN = N_GROUPS * 3 * DSW_WIDTH + DSW_WIDTH

REL_BUCKETS = 32
REL_MAX_DIST = 2048

kernel_name = "hybrid_gla_dilated_swa_adaln"


def rmsnorm(x, g):
    x32 = x.astype(jnp.float32)
    y = x32 * lax.rsqrt(jnp.mean(x32 * x32, axis=-1, keepdims=True) + NORM_EPS)
    return y.astype(x.dtype) * g


def t5_causal_bucket(n):
    max_exact = REL_BUCKETS // 2
    nf = np.maximum(n, 1).astype(np.float32)
    large = max_exact + (np.log(nf / max_exact) / math.log(REL_MAX_DIST / max_exact)
                         * (REL_BUCKETS - max_exact)).astype(np.int32)
    large = np.minimum(large, REL_BUCKETS - 1)
    return np.where(n < max_exact, n, large).astype(np.int32)


def gla_mixer(h, w_in, w_alpha, b_alpha, norm_g, w_out):
    B, S, _ = h.shape
    H, dk, dv, C = GLA_HEADS, GLA_DK_HEAD, GLA_DV_HEAD, GLA_CHUNK
    nc = S // C
    proj = h @ w_in
    q, k, v, g_lr, r = jnp.split(
        proj, [GLA_DK, 2 * GLA_DK, 2 * GLA_DK + GLA_DV, 2 * GLA_DK + GLA_DV + GLA_RANK], axis=-1)
    log_a = jax.nn.log_sigmoid((g_lr @ w_alpha + b_alpha).astype(jnp.float32)) / GLA_TAU

    def chunks(t, d):
        return t.astype(jnp.float32).reshape(B, nc, C, H, d).transpose(0, 3, 1, 2, 4)

    q = chunks(q, dk) * (dk ** -0.5)
    k = chunks(k, dk)
    v = chunks(v, dv)
    b = jnp.cumsum(chunks(log_a, dk), axis=3)

    mid = b[:, :, :, C // 2:C // 2 + 1]
    a_intra = jnp.einsum('bhnid,bhnjd->bhnij', q * jnp.exp(b - mid), k * jnp.exp(mid - b))
    causal = jnp.tril(jnp.ones((C, C), dtype=bool))
    a_intra = jnp.where(causal, a_intra, 0.0)
    o = jnp.einsum('bhnij,bhnje->bhnie', a_intra, v)

    b_last = b[:, :, :, -1:]
    kv = jnp.einsum('bhnjd,bhnje->bhnde', k * jnp.exp(b_last - b), v)
    decay = jnp.exp(b_last[:, :, :, 0])

    def step(state, inp):
        dec, kv_n = inp
        return dec[..., None] * state + kv_n, state

    _, states = lax.scan(step, jnp.zeros((B, H, dk, dv), jnp.float32),
                         (decay.transpose(2, 0, 1, 3), kv.transpose(2, 0, 1, 3, 4)))
    o = o + jnp.einsum('bhnid,nbhde->bhnie', q * jnp.exp(b), states)

    o = o.transpose(0, 2, 3, 1, 4).reshape(B, S, H, dv)
    o = rmsnorm(o, norm_g)
    o = o.reshape(B, S, GLA_DV).astype(h.dtype) * jax.nn.silu(r)
    return o @ w_out


def dilated_group_attn(q, k, v, rel_bias, window, dil):
    B, S, H, Dh = q.shape
    span = window // dil
    L = S // dil
    nb = -(-L // span)
    Lp = nb * span

    def to_sub(t):
        t = t.reshape(B, L, dil, H, Dh).transpose(0, 2, 3, 1, 4)
        return jnp.pad(t, ((0, 0), (0, 0), (0, 0), (0, Lp - L), (0, 0)))

    def banded(t):
        tp = jnp.pad(to_sub(t), ((0, 0), (0, 0), (0, 0), (span, 0), (0, 0)))
        prev = tp[:, :, :, :Lp].reshape(B, dil, H, nb, span, Dh)
        cur = tp[:, :, :, span:].reshape(B, dil, H, nb, span, Dh)
        return jnp.concatenate([prev, cur], axis=-2)

    qs = to_sub(q).reshape(B, dil, H, nb, span, Dh)
    kb, vb = banded(k), banded(v)

    qi = np.arange(span)[:, None]
    kj = np.arange(2 * span)[None, :]
    steps = qi + span - kj
    in_window = (steps >= 0) & (steps <= span)
    bucket = t5_causal_bucket(np.clip(steps, 0, span) * dil)
    bias = rel_bias[bucket].transpose(2, 0, 1).astype(jnp.float32)
    first_block = (np.arange(nb) == 0)[:, None, None]
    mask = in_window[None] & (~first_block | (kj >= span)[None])

    logits = jnp.einsum('bdhnqe,bdhnke->bdhnqk', qs, kb).astype(jnp.float32) * (Dh ** -0.5)
    logits = jnp.where(mask, logits + bias[:, None], -jnp.inf)
    lse = jax.nn.logsumexp(logits, axis=-1)
    p = jnp.exp(logits - lse[..., None])
    o = jnp.einsum('bdhnqk,bdhnke->bdhnqe', p.astype(vb.dtype), vb)

    o = o.reshape(B, dil, H, Lp, Dh)[:, :, :, :L].transpose(0, 3, 1, 2, 4).reshape(B, S, H, Dh)
    lse = lse.reshape(B, dil, H, Lp)[:, :, :, :L].transpose(0, 3, 1, 2).reshape(B, S, H)
    return o, lse


def dilated_mixer(h, w_in, w_out, rel_bias):
    B, S, _ = h.shape
    proj = h @ w_in
    qkv = proj[..., :N_GROUPS * 3 * DSW_WIDTH].reshape(B, S, N_GROUPS, 3, DSW_HEADS, DSW_HEAD_DIM)
    gate = proj[..., N_GROUPS * 3 * DSW_WIDTH:]
    outs, lses = [], []
    for g, (window, dil) in enumerate(DSW_GROUPS):
        o, lse = dilated_group_attn(qkv[:, :, g, 0], qkv[:, :, g, 1], qkv[:, :, g, 2],
                                    rel_bias, window, dil)
        outs.append(o)
        lses.append(lse)
    w = jax.nn.softmax(jnp.stack(lses, axis=0), axis=0)
    o = jnp.sum(w[..., None] * jnp.stack(outs, axis=0).astype(jnp.float32), axis=0)
    o = o.reshape(B, S, DSW_WIDTH).astype(h.dtype) * jax.nn.silu(gate)
    return o @ w_out


def setup_inputs(seed: int = 0) -> dict:
    key = jax.random.key(seed)
    ks = jax.random.split(key, 16)
    n_gla = (DEPTH + N_MIXERS - 1) // N_MIXERS
    n_dsw = DEPTH // N_MIXERS
    nrm = jax.random.normal
    f32 = jnp.float32
    return {
        "x": nrm(ks[0], (BATCH, SEQ, D_MODEL), f32),
        "c": nrm(ks[1], (BATCH, D_MODEL), f32),
        "ada_w": nrm(ks[2], (DEPTH, D_MODEL, 3 * D_MODEL), f32) * D_MODEL ** -0.5,
        "ada_b": nrm(ks[3], (DEPTH, 3 * D_MODEL), f32) * 0.02,
        "norm_g": 1.0 + 0.05 * nrm(ks[4], (DEPTH, D_MODEL), f32),
        "gla_w_in": nrm(ks[5], (n_gla, D_MODEL, GLA_IN), f32) * D_MODEL ** -0.5,
        "gla_w_alpha": nrm(ks[6], (n_gla, GLA_RANK, GLA_DK), f32) * GLA_RANK ** -0.5,
        "gla_b_alpha": nrm(ks[7], (n_gla, GLA_DK), f32) * 0.1,
        "gla_norm_g": 1.0 + 0.05 * nrm(ks[8], (n_gla, GLA_DV_HEAD), f32),
        "gla_w_out": nrm(ks[9], (n_gla, GLA_DV, D_MODEL), f32) * GLA_DV ** -0.5,
        "dsw_w_in": nrm(ks[10], (n_dsw, D_MODEL, DSW_IN), f32) * D_MODEL ** -0.5,
        "dsw_w_out": nrm(ks[11], (n_dsw, DSW_WIDTH, D_MODEL), f32) * DSW_WIDTH ** -0.5,
        "rel_bias": nrm(ks[12], (REL_BUCKETS, DSW_HEADS), f32) * 0.5,
        "final_g": 1.0 + 0.05 * nrm(ks[13], (D_MODEL,), f32),
    }


def reference(x, c, ada_w, ada_b, norm_g, gla_w_in, gla_w_alpha, gla_b_alpha, gla_norm_g,
              gla_w_out, dsw_w_in, dsw_w_out, rel_bias, final_g):
    c_act = jax.nn.silu(c)
    for i in range(DEPTH):
        mod = c_act @ ada_w[i] + ada_b[i]
        shift, scale, gate = jnp.split(mod, 3, axis=-1)
        h = rmsnorm(x, norm_g[i]) * (1.0 + scale[:, None]) + shift[:, None]
        j = i // N_MIXERS
        if i % N_MIXERS == 0:
            y = gla_mixer(h, gla_w_in[j], gla_w_alpha[j], gla_b_alpha[j], gla_norm_g[j], gla_w_out[j])
        else:
            y = dilated_mixer(h, dsw_w_in[j], dsw_w_out[j], rel_bias)
        x = x + gate[:, None] * y
    return rmsnorm(x, final_g)
```

```python
import functools
import math

import jax
import jax.numpy as jnp
import numpy as np
from jax import lax
from jax.experimental import pallas as pl
from jax.experimental.pallas import tpu as pltpu

F32 = jnp.float32
BF16 = jnp.bfloat16

D_MODEL = 1024
DEPTH = 2
NORM_EPS = 1e-6

GLA_HEADS = 4
GLA_DK = D_MODEL // 2
GLA_DV = D_MODEL
GLA_DK_HEAD = GLA_DK // GLA_HEADS
GLA_DV_HEAD = GLA_DV // GLA_HEADS
GLA_RANK = 16
GLA_TAU = 16.0
GLA_CHUNK = 64

DSW_HEADS = 16
DSW_HEAD_DIM = D_MODEL // DSW_HEADS
DSW_WIDTH = DSW_HEADS * DSW_HEAD_DIM
DSW_GROUPS = ((128, 1), (512, 4), (2048, 16))
N_GROUPS = len(DSW_GROUPS)
DSW_SPAN = 128
REL_BUCKETS = 32
REL_MAX_DIST = 2048

LANES = 128
HEADS_PER_STEP = LANES // DSW_HEAD_DIM
VMEM_LIMIT_BYTES = 56 * 1024 * 1024

GLA_ROW_TILE = 256
OUT_ROW_TILE = 512
PROJ_ROW_CHUNK = 256

_NT = (((1,), (1,)), ((), ()))


def _dot(a, b, precision=None):
    return jnp.dot(a, b, preferred_element_type=F32, precision=precision)


def _dot_nt(a, b):
    return lax.dot_general(a, b, _NT, preferred_element_type=F32)


def _silu(v):
    return v * (1.0 / (1.0 + jnp.exp(-v)))


def _log_sigmoid(z):
    return jnp.minimum(z, 0.0) - jnp.log(1.0 + jnp.exp(-jnp.abs(z)))


def _modulated_rmsnorm(x, g, shift, scale):
    y = x * lax.rsqrt(jnp.mean(x * x, axis=-1, keepdims=True) + NORM_EPS)
    return y * g * (1.0 + scale) + shift


def _adaln_kernel(c_ref, w_ref, b_ref, out_ref):
    c_act = _silu(c_ref[...])
    out_ref[...] = _dot(c_act, w_ref[...], precision=lax.Precision.HIGHEST) + b_ref[...]


def _adaln_mod(c, ada_w, ada_b):
    batch = c.shape[0]
    n_col = 3
    return pl.pallas_call(
        _adaln_kernel,
        grid=(DEPTH, n_col),
        in_specs=[
            pl.BlockSpec((batch, D_MODEL), lambda i, j: (0, 0)),
            pl.BlockSpec((None, D_MODEL, D_MODEL), lambda i, j: (i, 0, j)),
            pl.BlockSpec((None, 1, D_MODEL), lambda i, j: (i, 0, j)),
        ],
        out_specs=pl.BlockSpec((None, batch, D_MODEL), lambda i, j: (i, 0, j)),
        out_shape=jax.ShapeDtypeStruct((DEPTH, batch, 3 * D_MODEL), F32),
        compiler_params=pltpu.CompilerParams(
            dimension_semantics=("arbitrary", "arbitrary"), vmem_limit_bytes=VMEM_LIMIT_BYTES),
        name="adaln_mod",
    )(c, ada_w, ada_b.reshape(DEPTH, 1, 3 * D_MODEL))


def _gla_kernel(x_ref, mod_ref, ng_ref, wq_ref, wk_ref, wv_ref, wr_ref, wg_ref, wa_ref, ba_ref,
                hng_ref, wo_ref, out_ref, state_ref, q_scr, k_scr, v_scr, b_scr, o_scr):
    ts = GLA_ROW_TILE
    ch = GLA_CHUNK

    @pl.when(pl.program_id(1) == 0)
    def _():
        state_ref[...] = jnp.zeros_like(state_ref)

    x = x_ref[...]
    shift, scale, gate = mod_ref[0:1, :], mod_ref[1:2, :], mod_ref[2:3, :]
    hb = _modulated_rmsnorm(x, ng_ref[...], shift, scale).astype(BF16)

    q_scr[...] = _dot(hb, wq_ref[...]) * (GLA_DK_HEAD ** -0.5)
    k_scr[...] = _dot(hb, wk_ref[...])
    v_scr[...] = _dot(hb, wv_ref[...])
    g_lr = _dot(hb, wg_ref[...])
    z = _dot(g_lr, wa_ref[...], precision=lax.Precision.HIGHEST) + ba_ref[...]
    log_a = _log_sigmoid(z) / GLA_TAU
    row = lax.broadcasted_iota(jnp.int32, (ts, ts), 0)
    col = lax.broadcasted_iota(jnp.int32, (ts, ts), 1)
    tril = jnp.where((col <= row) & ((row // ch) == (col // ch)), 1.0, 0.0).astype(F32)
    b_scr[...] = _dot(tril, log_a, precision=lax.Precision.HIGHEST)

    ci = lax.broadcasted_iota(jnp.int32, (ch, ch), 0)
    cj = lax.broadcasted_iota(jnp.int32, (ch, ch), 1)
    causal = cj <= ci

    for c in range(ts // ch):
        rows = pl.ds(c * ch, ch)
        for hd in range(GLA_HEADS):
            kl = pl.ds(hd * GLA_DK_HEAD, GLA_DK_HEAD)
            vl = pl.ds(hd * GLA_DV_HEAD, GLA_DV_HEAD)
            bc = b_scr[rows, kl]
            mid = bc[ch // 2:ch // 2 + 1, :]
            last = bc[ch - 1:ch, :]
            qc = q_scr[rows, kl]
            kc = k_scr[rows, kl]
            vc = v_scr[rows, vl]
            vcb = vc.astype(BF16)
            qe = (qc * jnp.exp(bc - mid)).astype(BF16)
            ke = (kc * jnp.exp(mid - bc)).astype(BF16)
            a = jnp.where(causal, _dot_nt(qe, ke), 0.0)
            o = _dot(a.astype(BF16), vcb)
            st = state_ref[hd]
            qb = (qc * jnp.exp(bc)).astype(BF16)
            o = o + _dot_nt(qb, st.astype(BF16))
            kd = (kc * jnp.exp(last - bc)).astype(BF16)
            state_ref[hd] = st * jnp.exp(last) + _dot(vc.T.astype(BF16), kd)
            o_scr[rows, vl] = o

    r = _dot(hb, wr_ref[...])
    parts = []
    for hd in range(GLA_HEADS):
        oh = o_scr[:, pl.ds(hd * GLA_DV_HEAD, GLA_DV_HEAD)]
        yh = oh * lax.rsqrt(jnp.mean(oh * oh, axis=-1, keepdims=True) + NORM_EPS)
        parts.append(yh * hng_ref[...])
    og = (jnp.concatenate(parts, axis=-1) * _silu(r)).astype(BF16)
    out_ref[...] = x + gate * _dot(og, wo_ref[...])


def _gla_layer(x, mod, norm_g, w_in, w_alpha, b_alpha, head_norm_g, w_out):
    batch, seq, _ = x.shape
    ts = GLA_ROW_TILE
    o_q, o_k, o_v, o_g, o_r = 0, GLA_DK, 2 * GLA_DK, 2 * GLA_DK + GLA_DV, 2 * GLA_DK + GLA_DV + GLA_RANK
    w_in_b = w_in.astype(BF16)
    wq, wk, wv = w_in_b[:, o_q:o_k], w_in_b[:, o_k:o_v], w_in_b[:, o_v:o_g]
    wr = w_in_b[:, o_r:]
    wg = jnp.pad(w_in_b[:, o_g:o_r], ((0, 0), (0, LANES - GLA_RANK)))
    wa = jnp.pad(w_alpha, ((0, LANES - GLA_RANK), (0, 0)))

    def const(shape):
        return pl.BlockSpec(shape, lambda b, t: (0,) * len(shape))

    return pl.pallas_call(
        _gla_kernel,
        grid=(batch, seq // ts),
        in_specs=[
            pl.BlockSpec((None, ts, D_MODEL), lambda b, t: (b, t, 0)),
            pl.BlockSpec((None, 3, D_MODEL), lambda b, t: (b, 0, 0)),
            const((1, D_MODEL)),
            const((D_MODEL, GLA_DK)), const((D_MODEL, GLA_DK)), const((D_MODEL, GLA_DV)),
            const((D_MODEL, GLA_DV)), const((D_MODEL, LANES)),
            const((LANES, GLA_DK)), const((1, GLA_DK)),
            const((1, GLA_DV_HEAD)), const((GLA_DV, D_MODEL)),
        ],
        out_specs=pl.BlockSpec((None, ts, D_MODEL), lambda b, t: (b, t, 0)),
        out_shape=jax.ShapeDtypeStruct(x.shape, F32),
        scratch_shapes=[
            pltpu.VMEM((GLA_HEADS, GLA_DV_HEAD, GLA_DK_HEAD), F32),
            pltpu.VMEM((ts, GLA_DK), F32), pltpu.VMEM((ts, GLA_DK), F32),
            pltpu.VMEM((ts, GLA_DV), F32), pltpu.VMEM((ts, GLA_DK), F32),
            pltpu.VMEM((ts, GLA_DV), F32),
        ],
        compiler_params=pltpu.CompilerParams(
            dimension_semantics=("arbitrary", "arbitrary"), vmem_limit_bytes=VMEM_LIMIT_BYTES),
        name="gla_layer",
    )(x, mod, norm_g.reshape(1, D_MODEL), wq, wk, wv, wr, wg, wa, b_alpha.reshape(1, GLA_DK),
      head_norm_g.reshape(1, GLA_DV_HEAD), w_out.astype(BF16))


def _section_group(j):
    return jnp.where(j == 3 * N_GROUPS, 0, j // 3)


def _dsw_in_proj_kernel(x_ref, mod_ref, ng_ref, w_ref, out_ref, h_scr, hf_scr):
    seq = x_ref.shape[0]
    rc = PROJ_ROW_CHUNK
    n_lane_blocks = D_MODEL // LANES
    j = pl.program_id(1)

    @pl.when(j == 0)
    def _():
        shift, scale = mod_ref[0:1, :], mod_ref[1:2, :]
        g = ng_ref[...]
        for s in range(seq // rc):
            rows = pl.ds(s * rc, rc)
            h = _modulated_rmsnorm(x_ref[rows, :], g, shift, scale)
            h_scr[0, rows, :] = h.astype(BF16)
            for cb in range(n_lane_blocks):
                hf_scr[cb, rows, :] = h[:, cb * LANES:(cb + 1) * LANES]
        for gi, (_, dil) in enumerate(DSW_GROUPS):
            if dil == 1:
                continue
            seg = seq // dil
            n = min(rc, seg)
            for r in range(dil):
                for s in range(seg // n):
                    for cb in range(n_lane_blocks):
                        piece = hf_scr[cb, pl.ds(r + s * n * dil, n, stride=dil), :]
                        h_scr[gi, pl.ds(r * seg + s * n, n), pl.ds(cb * LANES, LANES)] = piece.astype(BF16)

    gi = _section_group(j)
    w = w_ref[...]
    for s in range(seq // rc):
        rows = pl.ds(s * rc, rc)
        out_ref[rows, :] = _dot(h_scr[gi, rows, :], w).astype(BF16)


def _dsw_in_proj(x, mod, norm_g, w_in):
    batch, seq, _ = x.shape
    n_sec = 3 * N_GROUPS + 1
    return pl.pallas_call(
        _dsw_in_proj_kernel,
        grid=(batch, n_sec),
        in_specs=[
            pl.BlockSpec((None, seq, D_MODEL), lambda b, j: (b, 0, 0)),
            pl.BlockSpec((None, 3, D_MODEL), lambda b, j: (b, 0, 0)),
            pl.BlockSpec((1, D_MODEL), lambda b, j: (0, 0)),
            pl.BlockSpec((D_MODEL, DSW_WIDTH), lambda b, j: (0, j)),
        ],
        out_specs=pl.BlockSpec((None, None, seq, DSW_WIDTH), lambda b, j: (b, j, 0, 0)),
        out_shape=jax.ShapeDtypeStruct((batch, n_sec, seq, DSW_WIDTH), BF16),
        scratch_shapes=[pltpu.VMEM((N_GROUPS, seq, D_MODEL), BF16),
                        pltpu.VMEM((D_MODEL // LANES, seq, LANES), F32)],
        compiler_params=pltpu.CompilerParams(
            dimension_semantics=("arbitrary", "arbitrary"), vmem_limit_bytes=VMEM_LIMIT_BYTES),
        name="dsw_in_proj",
    )(x, mod, norm_g.reshape(1, D_MODEL), w_in.astype(BF16))


def _t5_causal_bucket(n):
    max_exact = REL_BUCKETS // 2
    nf = np.maximum(n, 1).astype(np.float32)
    large = max_exact + (np.log(nf / max_exact) / math.log(REL_MAX_DIST / max_exact)
                         * (REL_BUCKETS - max_exact)).astype(np.int32)
    large = np.minimum(large, REL_BUCKETS - 1)
    return np.where(n < max_exact, n, large).astype(np.int32)


def _bucket_tables():
    qi = np.arange(DSW_SPAN)[:, None]
    kj = np.arange(2 * DSW_SPAN)[None, :]
    steps = qi + DSW_SPAN - kj
    in_window = (steps >= 0) & (steps <= DSW_SPAN)
    tables = []
    for _, dil in DSW_GROUPS:
        bucket = _t5_causal_bucket(np.clip(steps, 0, DSW_SPAN) * dil)
        tables.append(np.where(in_window, bucket, -1))
    return np.stack(tables).astype(np.int32)


def _dsw_bias_kernel(rel_ref, bucket_ref, out_ref):
    bucket = bucket_ref[...]
    for h in range(DSW_HEADS):
        acc = jnp.where(bucket < 0, -jnp.inf, 0.0).astype(F32)
        for bk in range(REL_BUCKETS):
            acc = jnp.where(bucket == bk, rel_ref[bk, h], acc)
        out_ref[h] = acc


def _dsw_bias(rel_bias):
    return pl.pallas_call(
        _dsw_bias_kernel,
        grid=(N_GROUPS,),
        in_specs=[
            pl.BlockSpec(memory_space=pltpu.SMEM),
            pl.BlockSpec((None, DSW_SPAN, 2 * DSW_SPAN), lambda g: (g, 0, 0)),
        ],
        out_specs=pl.BlockSpec((None, DSW_HEADS, DSW_SPAN, 2 * DSW_SPAN), lambda g: (g, 0, 0, 0)),
        out_shape=jax.ShapeDtypeStruct((N_GROUPS, DSW_HEADS, DSW_SPAN, 2 * DSW_SPAN), F32),
        compiler_params=pltpu.CompilerParams(dimension_semantics=("arbitrary",)),
        name="dsw_bias",
    )(rel_bias, jnp.asarray(_bucket_tables()))


def _dsw_attention_kernel(*refs):
    qkv_refs = refs[:3 * N_GROUPS]
    gate_ref, bias_ref, out_ref, o_scr, lse_scr = refs[3 * N_GROUPS:]
    seq = out_ref.shape[0]
    span = DSW_SPAN
    lane = lax.broadcasted_iota(jnp.int32, (span, LANES), 1)
    head_masks = [(lane // DSW_HEAD_DIM) == hh for hh in range(HEADS_PER_STEP)]

    for gi, (_, dil) in enumerate(DSW_GROUPS):
        q_ref, k_ref, v_ref = qkv_refs[3 * gi:3 * gi + 3]
        blocks_per_class = seq // dil // span
        for blk in range(seq // span):
            first = blk % blocks_per_class == 0
            q = q_ref[pl.ds(blk * span, span), :]
            if first:
                keys = pl.ds(blk * span, span)
            else:
                keys = pl.ds((blk - 1) * span, 2 * span)
            k = k_ref[keys, :]
            v = v_ref[keys, :]
            o_blk = jnp.zeros((span, LANES), F32)
            lse_blk = jnp.zeros((span, LANES), F32)
            for hh in range(HEADS_PER_STEP):
                qm = jnp.where(head_masks[hh], q, jnp.zeros_like(q))
                bias = bias_ref[gi, hh, :, pl.ds(span, span)] if first else bias_ref[gi, hh]
                s = _dot_nt(qm, k) * (DSW_HEAD_DIM ** -0.5) + bias
                m = jnp.max(s, axis=-1, keepdims=True)
                e = jnp.exp(s - m)
                l = jnp.sum(e, axis=-1, keepdims=True)
                pv = _dot(e.astype(BF16), v)
                o_blk = jnp.where(head_masks[hh], pv / l, o_blk)
                lse_blk = jnp.where(head_masks[hh], m + jnp.log(l), lse_blk)
            r = blk // blocks_per_class
            i0 = (blk % blocks_per_class) * span
            dst = pl.ds(i0 * dil + r, span, stride=dil) if dil > 1 else pl.ds(blk * span, span)
            o_scr[gi, dst, :] = o_blk
            lse_scr[gi, dst, :] = lse_blk

    lse = [lse_scr[gi] for gi in range(N_GROUPS)]
    top = functools.reduce(jnp.maximum, lse)
    wts = [jnp.exp(v - top) for v in lse]
    total = functools.reduce(lambda a, b: a + b, wts)
    acc = functools.reduce(lambda a, b: a + b, [wts[gi] * o_scr[gi] for gi in range(N_GROUPS)])
    out_ref[...] = ((acc / total) * _silu(gate_ref[...].astype(F32))).astype(BF16)


def _dsw_attention(proj, bias):
    batch, n_sec, seq, _ = proj.shape
    n_steps = DSW_HEADS // HEADS_PER_STEP

    def sec(j):
        return pl.BlockSpec((None, None, seq, LANES), lambda b, hp, j=j: (b, j, 0, hp))

    return pl.pallas_call(
        _dsw_attention_kernel,
        grid=(batch, n_steps),
        in_specs=[sec(j) for j in range(n_sec)] + [
            pl.BlockSpec((N_GROUPS, HEADS_PER_STEP, DSW_SPAN, 2 * DSW_SPAN), lambda b, hp: (0, hp, 0, 0)),
        ],
        out_specs=pl.BlockSpec((None, seq, LANES), lambda b, hp: (b, 0, hp)),
        out_shape=jax.ShapeDtypeStruct((batch, seq, DSW_WIDTH), BF16),
        scratch_shapes=[pltpu.VMEM((N_GROUPS, seq, LANES), F32), pltpu.VMEM((N_GROUPS, seq, LANES), F32)],
        compiler_params=pltpu.CompilerParams(
            dimension_semantics=("arbitrary", "arbitrary"), vmem_limit_bytes=VMEM_LIMIT_BYTES),
        name="dsw_attention",
    )(*([proj] * n_sec), bias)


def _dsw_out_proj_kernel(o_ref, x_ref, mod_ref, w_ref, fg_ref, out_ref):
    gate = mod_ref[2:3, :]
    xn = x_ref[...] + gate * _dot(o_ref[...], w_ref[...])
    out_ref[...] = xn * lax.rsqrt(jnp.mean(xn * xn, axis=-1, keepdims=True) + NORM_EPS) * fg_ref[...]


def _dsw_out_proj(og, x, mod, w_out, final_g):
    batch, seq, _ = x.shape
    ts = OUT_ROW_TILE
    return pl.pallas_call(
        _dsw_out_proj_kernel,
        grid=(batch, seq // ts),
        in_specs=[
            pl.BlockSpec((None, ts, DSW_WIDTH), lambda b, t: (b, t, 0)),
            pl.BlockSpec((None, ts, D_MODEL), lambda b, t: (b, t, 0)),
            pl.BlockSpec((None, 3, D_MODEL), lambda b, t: (b, 0, 0)),
            pl.BlockSpec((DSW_WIDTH, D_MODEL), lambda b, t: (0, 0)),
            pl.BlockSpec((1, D_MODEL), lambda b, t: (0, 0)),
        ],
        out_specs=pl.BlockSpec((None, ts, D_MODEL), lambda b, t: (b, t, 0)),
        out_shape=jax.ShapeDtypeStruct(x.shape, F32),
        compiler_params=pltpu.CompilerParams(
            dimension_semantics=("arbitrary", "arbitrary"), vmem_limit_bytes=VMEM_LIMIT_BYTES),
        name="dsw_out_proj",
    )(og, x, mod, w_out.astype(BF16), final_g.reshape(1, D_MODEL))


def kernel(x, c, ada_w, ada_b, norm_g, gla_w_in, gla_w_alpha, gla_b_alpha, gla_norm_g, gla_w_out,
           dsw_w_in, dsw_w_out, rel_bias, final_g):
    batch = x.shape[0]
    mod = _adaln_mod(c, ada_w, ada_b)
    mod = mod.reshape(DEPTH, batch, 3, D_MODEL)
    x = _gla_layer(x, mod[0], norm_g[0], gla_w_in[0], gla_w_alpha[0], gla_b_alpha[0], gla_norm_g[0],
                   gla_w_out[0])
    proj = _dsw_in_proj(x, mod[1], norm_g[1], dsw_w_in[0])
    bias = _dsw_bias(rel_bias)
    og = _dsw_attention(proj, bias)
    return _dsw_out_proj(og, x, mod[1], dsw_w_out[0], final_g)
```

```python
import functools
import math

import jax
import jax.numpy as jnp
import numpy as np
from jax import lax
from jax.experimental import pallas as pl
from jax.experimental.pallas import tpu as pltpu

F32 = jnp.float32
BF16 = jnp.bfloat16

D_MODEL = 1024
DEPTH = 2
NORM_EPS = 1e-6

GLA_HEADS = 4
GLA_DK = D_MODEL // 2
GLA_DV = D_MODEL
GLA_DK_HEAD = GLA_DK // GLA_HEADS
GLA_DV_HEAD = GLA_DV // GLA_HEADS
GLA_RANK = 16
GLA_TAU = 16.0
GLA_CHUNK = 64

DSW_HEADS = 16
DSW_HEAD_DIM = D_MODEL // DSW_HEADS
DSW_WIDTH = DSW_HEADS * DSW_HEAD_DIM
DSW_GROUPS = ((128, 1), (512, 4), (2048, 16))
N_GROUPS = len(DSW_GROUPS)
DSW_SPAN = 128
REL_BUCKETS = 32
REL_MAX_DIST = 2048

LANES = 128
HEADS_PER_STEP = LANES // DSW_HEAD_DIM
assert HEADS_PER_STEP == 2
LOG2E = math.log2(math.e)
QUERY_SCALE = DSW_HEAD_DIM ** -0.5 * LOG2E
VMEM_LIMIT_BYTES = 56 * 1024 * 1024

GLA_ROW_TILE = 256
OUT_ROW_TILE = 512
PROJ_ROW_CHUNK = 256

_NT = (((1,), (1,)), ((), ()))


def _dot(a, b, precision=None):
    return jnp.dot(a, b, preferred_element_type=F32, precision=precision)


def _dot_nt(a, b):
    return lax.dot_general(a, b, _NT, preferred_element_type=F32)


def _silu(v):
    return v * (1.0 / (1.0 + jnp.exp(-v)))


def _log_sigmoid(z):
    return jnp.minimum(z, 0.0) - jnp.log(1.0 + jnp.exp(-jnp.abs(z)))


def _modulated_rmsnorm(x, g, shift, scale):
    y = x * lax.rsqrt(jnp.mean(x * x, axis=-1, keepdims=True) + NORM_EPS)
    return y * g * (1.0 + scale) + shift


def _adaln_kernel(c_ref, w_ref, b_ref, out_ref):
    c_act = _silu(c_ref[...])
    out_ref[...] = _dot(c_act, w_ref[...], precision=lax.Precision.HIGHEST) + b_ref[...]


def _adaln_mod(c, ada_w, ada_b):
    batch = c.shape[0]
    n_col = 3
    return pl.pallas_call(
        _adaln_kernel,
        grid=(DEPTH, n_col),
        in_specs=[
            pl.BlockSpec((batch, D_MODEL), lambda i, j: (0, 0)),
            pl.BlockSpec((None, D_MODEL, D_MODEL), lambda i, j: (i, 0, j)),
            pl.BlockSpec((None, 1, D_MODEL), lambda i, j: (i, 0, j)),
        ],
        out_specs=pl.BlockSpec((None, batch, D_MODEL), lambda i, j: (i, 0, j)),
        out_shape=jax.ShapeDtypeStruct((DEPTH, batch, 3 * D_MODEL), F32),
        compiler_params=pltpu.CompilerParams(
            dimension_semantics=("arbitrary", "arbitrary"), vmem_limit_bytes=VMEM_LIMIT_BYTES),
        name="adaln_mod",
    )(c, ada_w, ada_b.reshape(DEPTH, 1, 3 * D_MODEL))


def _gla_kernel(x_ref, mod_ref, ng_ref, wq_ref, wk_ref, wv_ref, wr_ref, wg_ref, wa_ref, ba_ref,
                hng_ref, wo_ref, out_ref, state_ref, q_scr, k_scr, v_scr, b_scr, o_scr):
    ts = GLA_ROW_TILE
    ch = GLA_CHUNK

    @pl.when(pl.program_id(1) == 0)
    def _():
        state_ref[...] = jnp.zeros_like(state_ref)

    x = x_ref[...]
    shift, scale, gate = mod_ref[0:1, :], mod_ref[1:2, :], mod_ref[2:3, :]
    hb = _modulated_rmsnorm(x, ng_ref[...], shift, scale).astype(BF16)

    q_scr[...] = _dot(hb, wq_ref[...]) * (GLA_DK_HEAD ** -0.5)
    k_scr[...] = _dot(hb, wk_ref[...])
    v_scr[...] = _dot(hb, wv_ref[...])
    g_lr = _dot(hb, wg_ref[...])
    z = _dot(g_lr, wa_ref[...], precision=lax.Precision.HIGHEST) + ba_ref[...]
    log_a = _log_sigmoid(z) / GLA_TAU
    row = lax.broadcasted_iota(jnp.int32, (ts, ts), 0)
    col = lax.broadcasted_iota(jnp.int32, (ts, ts), 1)
    tril = jnp.where((col <= row) & ((row // ch) == (col // ch)), 1.0, 0.0).astype(F32)
    b_scr[...] = _dot(tril, log_a, precision=lax.Precision.HIGHEST)

    ci = lax.broadcasted_iota(jnp.int32, (ch, ch), 0)
    cj = lax.broadcasted_iota(jnp.int32, (ch, ch), 1)
    causal = cj <= ci

    for c in range(ts // ch):
        rows = pl.ds(c * ch, ch)
        for hd in range(GLA_HEADS):
            kl = pl.ds(hd * GLA_DK_HEAD, GLA_DK_HEAD)
            vl = pl.ds(hd * GLA_DV_HEAD, GLA_DV_HEAD)
            bc = b_scr[rows, kl]
            mid = bc[ch // 2:ch // 2 + 1, :]
            last = bc[ch - 1:ch, :]
            qc = q_scr[rows, kl]
            kc = k_scr[rows, kl]
            vc = v_scr[rows, vl]
            vcb = vc.astype(BF16)
            qe = (qc * jnp.exp(bc - mid)).astype(BF16)
            ke = (kc * jnp.exp(mid - bc)).astype(BF16)
            a = jnp.where(causal, _dot_nt(qe, ke), 0.0)
            o = _dot(a.astype(BF16), vcb)
            st = state_ref[hd]
            qb = (qc * jnp.exp(bc)).astype(BF16)
            o = o + _dot_nt(qb, st.astype(BF16))
            kd = (kc * jnp.exp(last - bc)).astype(BF16)
            state_ref[hd] = st * jnp.exp(last) + _dot(vc.T.astype(BF16), kd)
            o_scr[rows, vl] = o

    r = _dot(hb, wr_ref[...])
    parts = []
    for hd in range(GLA_HEADS):
        oh = o_scr[:, pl.ds(hd * GLA_DV_HEAD, GLA_DV_HEAD)]
        yh = oh * lax.rsqrt(jnp.mean(oh * oh, axis=-1, keepdims=True) + NORM_EPS)
        parts.append(yh * hng_ref[...])
    og = (jnp.concatenate(parts, axis=-1) * _silu(r)).astype(BF16)
    out_ref[...] = x + gate * _dot(og, wo_ref[...])


def _gla_layer(x, mod, norm_g, w_in, w_alpha, b_alpha, head_norm_g, w_out):
    batch, seq, _ = x.shape
    ts = GLA_ROW_TILE
    o_q, o_k, o_v, o_g, o_r = 0, GLA_DK, 2 * GLA_DK, 2 * GLA_DK + GLA_DV, 2 * GLA_DK + GLA_DV + GLA_RANK
    w_in_b = w_in.astype(BF16)
    wq, wk, wv = w_in_b[:, o_q:o_k], w_in_b[:, o_k:o_v], w_in_b[:, o_v:o_g]
    wr = w_in_b[:, o_r:]
    wg = jnp.pad(w_in_b[:, o_g:o_r], ((0, 0), (0, LANES - GLA_RANK)))
    wa = jnp.pad(w_alpha, ((0, LANES - GLA_RANK), (0, 0)))

    def const(shape):
        return pl.BlockSpec(shape, lambda b, t: (0,) * len(shape))

    return pl.pallas_call(
        _gla_kernel,
        grid=(batch, seq // ts),
        in_specs=[
            pl.BlockSpec((None, ts, D_MODEL), lambda b, t: (b, t, 0)),
            pl.BlockSpec((None, 3, D_MODEL), lambda b, t: (b, 0, 0)),
            const((1, D_MODEL)),
            const((D_MODEL, GLA_DK)), const((D_MODEL, GLA_DK)), const((D_MODEL, GLA_DV)),
            const((D_MODEL, GLA_DV)), const((D_MODEL, LANES)),
            const((LANES, GLA_DK)), const((1, GLA_DK)),
            const((1, GLA_DV_HEAD)), const((GLA_DV, D_MODEL)),
        ],
        out_specs=pl.BlockSpec((None, ts, D_MODEL), lambda b, t: (b, t, 0)),
        out_shape=jax.ShapeDtypeStruct(x.shape, F32),
        scratch_shapes=[
            pltpu.VMEM((GLA_HEADS, GLA_DV_HEAD, GLA_DK_HEAD), F32),
            pltpu.VMEM((ts, GLA_DK), F32), pltpu.VMEM((ts, GLA_DK), F32),
            pltpu.VMEM((ts, GLA_DV), F32), pltpu.VMEM((ts, GLA_DK), F32),
            pltpu.VMEM((ts, GLA_DV), F32),
        ],
        compiler_params=pltpu.CompilerParams(
            dimension_semantics=("arbitrary", "arbitrary"), vmem_limit_bytes=VMEM_LIMIT_BYTES),
        name="gla_layer",
    )(x, mod, norm_g.reshape(1, D_MODEL), wq, wk, wv, wr, wg, wa, b_alpha.reshape(1, GLA_DK),
      head_norm_g.reshape(1, GLA_DV_HEAD), w_out.astype(BF16))


def _section_group(j):
    return jnp.where(j == 3 * N_GROUPS, 0, j // 3)


def _dsw_in_proj_kernel(x_ref, mod_ref, ng_ref, w_ref, out_ref, h_scr, hf_scr):
    seq = x_ref.shape[0]
    rc = PROJ_ROW_CHUNK
    n_lane_blocks = D_MODEL // LANES
    j = pl.program_id(1)

    @pl.when(j == 0)
    def _():
        shift, scale = mod_ref[0:1, :], mod_ref[1:2, :]
        g = ng_ref[...]
        for s in range(seq // rc):
            rows = pl.ds(s * rc, rc)
            h = _modulated_rmsnorm(x_ref[rows, :], g, shift, scale)
            h_scr[0, rows, :] = h.astype(BF16)
            for cb in range(n_lane_blocks):
                hf_scr[cb, rows, :] = h[:, cb * LANES:(cb + 1) * LANES]
        for gi, (_, dil) in enumerate(DSW_GROUPS):
            if dil == 1:
                continue
            seg = seq // dil
            n = min(rc, seg)
            for r in range(dil):
                for s in range(seg // n):
                    for cb in range(n_lane_blocks):
                        piece = hf_scr[cb, pl.ds(r + s * n * dil, n, stride=dil), :]
                        h_scr[gi, pl.ds(r * seg + s * n, n), pl.ds(cb * LANES, LANES)] = piece.astype(BF16)

    gi = _section_group(j)
    w = w_ref[...]
    is_query = (j % 3 == 0) & (j < 3 * N_GROUPS)
    out_scale = jnp.where(is_query, QUERY_SCALE, 1.0).astype(F32)
    for s in range(seq // rc):
        rows = pl.ds(s * rc, rc)
        out_ref[rows, :] = (_dot(h_scr[gi, rows, :], w) * out_scale).astype(BF16)


def _dsw_in_proj(x, mod, norm_g, w_in):
    batch, seq, _ = x.shape
    n_sec = 3 * N_GROUPS + 1
    return pl.pallas_call(
        _dsw_in_proj_kernel,
        grid=(batch, n_sec),
        in_specs=[
            pl.BlockSpec((None, seq, D_MODEL), lambda b, j: (b, 0, 0)),
            pl.BlockSpec((None, 3, D_MODEL), lambda b, j: (b, 0, 0)),
            pl.BlockSpec((1, D_MODEL), lambda b, j: (0, 0)),
            pl.BlockSpec((D_MODEL, DSW_WIDTH), lambda b, j: (0, j)),
        ],
        out_specs=pl.BlockSpec((None, None, seq, DSW_WIDTH), lambda b, j: (b, j, 0, 0)),
        out_shape=jax.ShapeDtypeStruct((batch, n_sec, seq, DSW_WIDTH), BF16),
        scratch_shapes=[pltpu.VMEM((N_GROUPS, seq, D_MODEL), BF16),
                        pltpu.VMEM((D_MODEL // LANES, seq, LANES), F32)],
        compiler_params=pltpu.CompilerParams(
            dimension_semantics=("arbitrary", "arbitrary"), vmem_limit_bytes=VMEM_LIMIT_BYTES),
        name="dsw_in_proj",
    )(x, mod, norm_g.reshape(1, D_MODEL), w_in.astype(BF16))


def _t5_causal_bucket(n):
    max_exact = REL_BUCKETS // 2
    nf = np.maximum(n, 1).astype(np.float32)
    large = max_exact + (np.log(nf / max_exact) / math.log(REL_MAX_DIST / max_exact)
                         * (REL_BUCKETS - max_exact)).astype(np.int32)
    large = np.minimum(large, REL_BUCKETS - 1)
    return np.where(n < max_exact, n, large).astype(np.int32)


def _bucket_tables():
    qi = np.arange(DSW_SPAN)[:, None]
    kj = np.arange(2 * DSW_SPAN)[None, :]
    steps = qi + DSW_SPAN - kj
    in_window = (steps >= 0) & (steps <= DSW_SPAN)
    tables = []
    for _, dil in DSW_GROUPS:
        bucket = _t5_causal_bucket(np.clip(steps, 0, DSW_SPAN) * dil)
        tables.append(np.where(in_window, bucket, -1))
    return np.stack(tables).astype(np.int32)


def _dsw_bias_kernel(rel_ref, bucket_ref, out_ref):
    bucket = bucket_ref[...]
    for h in range(DSW_HEADS):
        acc = jnp.where(bucket < 0, -jnp.inf, 0.0).astype(F32)
        for bk in range(REL_BUCKETS):
            acc = jnp.where(bucket == bk, rel_ref[bk, h], acc)
        out_ref[pl.ds(h * DSW_SPAN, DSW_SPAN), :] = acc * LOG2E


def _dsw_bias(rel_bias):
    return pl.pallas_call(
        _dsw_bias_kernel,
        grid=(N_GROUPS,),
        in_specs=[
            pl.BlockSpec(memory_space=pltpu.SMEM),
            pl.BlockSpec((None, DSW_SPAN, 2 * DSW_SPAN), lambda g: (g, 0, 0)),
        ],
        out_specs=pl.BlockSpec((None, DSW_HEADS * DSW_SPAN, 2 * DSW_SPAN), lambda g: (g, 0, 0)),
        out_shape=jax.ShapeDtypeStruct((N_GROUPS, DSW_HEADS * DSW_SPAN, 2 * DSW_SPAN), F32),
        compiler_params=pltpu.CompilerParams(dimension_semantics=("arbitrary",)),
        name="dsw_bias",
    )(rel_bias, jnp.asarray(_bucket_tables()))


def _dsw_attention_kernel(*refs):
    qkv_refs = refs[:3 * N_GROUPS]
    gate_ref, bias_ref, out_ref, o_scr, lse_scr = refs[3 * N_GROUPS:]
    seq = out_ref.shape[0]
    span = DSW_SPAN
    lane = lax.broadcasted_iota(jnp.int32, (span, LANES), 1)
    head_masks = [(lane // DSW_HEAD_DIM) == hh for hh in range(HEADS_PER_STEP)]

    units = [(gi, blk) for gi in range(N_GROUPS) for blk in range(seq // span)]

    def is_first(gi, blk):
        return blk % (seq // DSW_GROUPS[gi][1] // span) == 0

    def key_rows(gi, blk):
        return pl.ds(blk * span, span) if is_first(gi, blk) else pl.ds((blk - 1) * span, 2 * span)

    def scores(gi, blk):
        q = qkv_refs[3 * gi][pl.ds(blk * span, span), :]
        q2 = jnp.concatenate([jnp.where(mk, q, jnp.zeros_like(q)) for mk in head_masks], axis=0)
        return _dot_nt(q2, qkv_refs[3 * gi + 1][key_rows(gi, blk), :])

    s_next = scores(*units[0])
    for idx, (gi, blk) in enumerate(units):
        dil = DSW_GROUPS[gi][1]
        s = s_next
        if idx + 1 < len(units):
            s_next = scores(*units[idx + 1])
        if is_first(gi, blk):
            s = s + bias_ref[gi, :, pl.ds(span, span)]
        else:
            s = s + bias_ref[gi]
        m = jnp.max(s, axis=-1, keepdims=True)
        e = jnp.exp2(s - m)
        l = jnp.sum(e, axis=-1, keepdims=True)
        pv = _dot(e.astype(BF16), qkv_refs[3 * gi + 2][key_rows(gi, blk), :]) / l
        lse2 = m + jnp.log2(l)
        o_blk = jnp.where(head_masks[0], pv[:span], pv[span:])
        lse_blk = jnp.where(head_masks[0], lse2[:span], lse2[span:])
        blocks_per_class = seq // dil // span
        r = blk // blocks_per_class
        i0 = (blk % blocks_per_class) * span
        dst = pl.ds(i0 * dil + r, span, stride=dil) if dil > 1 else pl.ds(blk * span, span)
        o_scr[gi, dst, :] = o_blk
        lse_scr[gi, dst, :] = lse_blk

    lse = [lse_scr[gi] for gi in range(N_GROUPS)]
    top = functools.reduce(jnp.maximum, lse)
    wts = [jnp.exp2(v - top) for v in lse]
    total = functools.reduce(lambda a, b: a + b, wts)
    acc = functools.reduce(lambda a, b: a + b, [wts[gi] * o_scr[gi] for gi in range(N_GROUPS)])
    out_ref[...] = ((acc / total) * _silu(gate_ref[...].astype(F32))).astype(BF16)


def _dsw_attention(proj, bias):
    batch, n_sec, seq, _ = proj.shape
    n_steps = DSW_HEADS // HEADS_PER_STEP

    def sec(j):
        return pl.BlockSpec((None, None, seq, LANES), lambda b, hp, j=j: (b, j, 0, hp))

    return pl.pallas_call(
        _dsw_attention_kernel,
        grid=(batch, n_steps),
        in_specs=[sec(j) for j in range(n_sec)] + [
            pl.BlockSpec((N_GROUPS, HEADS_PER_STEP * DSW_SPAN, 2 * DSW_SPAN), lambda b, hp: (0, hp, 0)),
        ],
        out_specs=pl.BlockSpec((None, seq, LANES), lambda b, hp: (b, 0, hp)),
        out_shape=jax.ShapeDtypeStruct((batch, seq, DSW_WIDTH), BF16),
        scratch_shapes=[pltpu.VMEM((N_GROUPS, seq, LANES), F32), pltpu.VMEM((N_GROUPS, seq, LANES), F32)],
        compiler_params=pltpu.CompilerParams(
            dimension_semantics=("arbitrary", "arbitrary"), vmem_limit_bytes=VMEM_LIMIT_BYTES),
        name="dsw_attention",
    )(*([proj] * n_sec), bias)


def _dsw_out_proj_kernel(o_ref, x_ref, mod_ref, w_ref, fg_ref, out_ref):
    gate = mod_ref[2:3, :]
    xn = x_ref[...] + gate * _dot(o_ref[...], w_ref[...])
    out_ref[...] = xn * lax.rsqrt(jnp.mean(xn * xn, axis=-1, keepdims=True) + NORM_EPS) * fg_ref[...]


def _dsw_out_proj(og, x, mod, w_out, final_g):
    batch, seq, _ = x.shape
    ts = OUT_ROW_TILE
    return pl.pallas_call(
        _dsw_out_proj_kernel,
        grid=(batch, seq // ts),
        in_specs=[
            pl.BlockSpec((None, ts, DSW_WIDTH), lambda b, t: (b, t, 0)),
            pl.BlockSpec((None, ts, D_MODEL), lambda b, t: (b, t, 0)),
            pl.BlockSpec((None, 3, D_MODEL), lambda b, t: (b, 0, 0)),
            pl.BlockSpec((DSW_WIDTH, D_MODEL), lambda b, t: (0, 0)),
            pl.BlockSpec((1, D_MODEL), lambda b, t: (0, 0)),
        ],
        out_specs=pl.BlockSpec((None, ts, D_MODEL), lambda b, t: (b, t, 0)),
        out_shape=jax.ShapeDtypeStruct(x.shape, F32),
        compiler_params=pltpu.CompilerParams(
            dimension_semantics=("arbitrary", "arbitrary"), vmem_limit_bytes=VMEM_LIMIT_BYTES),
        name="dsw_out_proj",
    )(og, x, mod, w_out.astype(BF16), final_g.reshape(1, D_MODEL))


def kernel(x, c, ada_w, ada_b, norm_g, gla_w_in, gla_w_alpha, gla_b_alpha, gla_norm_g, gla_w_out,
           dsw_w_in, dsw_w_out, rel_bias, final_g):
    batch = x.shape[0]
    mod = _adaln_mod(c, ada_w, ada_b)
    mod = mod.reshape(DEPTH, batch, 3, D_MODEL)
    x = _gla_layer(x, mod[0], norm_g[0], gla_w_in[0], gla_w_alpha[0], gla_b_alpha[0], gla_norm_g[0],
                   gla_w_out[0])
    proj = _dsw_in_proj(x, mod[1], norm_g[1], dsw_w_in[0])
    bias = _dsw_bias(rel_bias)
    og = _dsw_attention(proj, bias)
    return _dsw_out_proj(og, x, mod[1], dsw_w_out[0], final_g)
```

```python
import functools
import math

import jax
import jax.numpy as jnp
import numpy as np
from jax import lax
from jax.experimental import pallas as pl
from jax.experimental.pallas import tpu as pltpu

F32 = jnp.float32
BF16 = jnp.bfloat16

D_MODEL = 1024
DEPTH = 2
NORM_EPS = 1e-6

GLA_HEADS = 4
GLA_DK = D_MODEL // 2
GLA_DV = D_MODEL
GLA_DK_HEAD = GLA_DK // GLA_HEADS
GLA_DV_HEAD = GLA_DV // GLA_HEADS
GLA_RANK = 16
GLA_TAU = 16.0
GLA_CHUNK = 64

DSW_HEADS = 16
DSW_HEAD_DIM = D_MODEL // DSW_HEADS
DSW_WIDTH = DSW_HEADS * DSW_HEAD_DIM
DSW_GROUPS = ((128, 1), (512, 4), (2048, 16))
N_GROUPS = len(DSW_GROUPS)
DSW_SPAN = 128
REL_BUCKETS = 32
REL_MAX_DIST = 2048

LANES = 128
HEADS_PER_STEP = LANES // DSW_HEAD_DIM
assert HEADS_PER_STEP == 2
LOG2E = math.log2(math.e)
QUERY_SCALE = DSW_HEAD_DIM ** -0.5 * LOG2E
VMEM_LIMIT_BYTES = 56 * 1024 * 1024

GLA_ROW_TILE = 256
OUT_ROW_TILE = 512
PROJ_ROW_CHUNK = 256

_NT = (((1,), (1,)), ((), ()))


def _dot(a, b, precision=None):
    return jnp.dot(a, b, preferred_element_type=F32, precision=precision)


def _dot_nt(a, b):
    return lax.dot_general(a, b, _NT, preferred_element_type=F32)


def _silu(v):
    return v * (1.0 / (1.0 + jnp.exp(-v)))


def _log_sigmoid(z):
    return jnp.minimum(z, 0.0) - jnp.log(1.0 + jnp.exp(-jnp.abs(z)))


def _chunk_cumsum(x, chunk):
    pos = lax.broadcasted_iota(jnp.int32, x.shape, 0) % chunk
    step = 1
    while step < chunk:
        x = x + jnp.where(pos >= step, pltpu.roll(x, shift=step, axis=0), 0.0)
        step *= 2
    return x


def _modulated_rmsnorm(x, g, shift, scale):
    y = x * lax.rsqrt(jnp.mean(x * x, axis=-1, keepdims=True) + NORM_EPS)
    return y * g * (1.0 + scale) + shift


def _adaln_kernel(c_ref, w_ref, b_ref, out_ref):
    c_act = _silu(c_ref[...])
    out_ref[...] = _dot(c_act, w_ref[...], precision=lax.Precision.HIGHEST) + b_ref[...]


def _adaln_mod(c, ada_w, ada_b):
    batch = c.shape[0]
    n_col = 3
    return pl.pallas_call(
        _adaln_kernel,
        grid=(DEPTH, n_col),
        in_specs=[
            pl.BlockSpec((batch, D_MODEL), lambda i, j: (0, 0)),
            pl.BlockSpec((None, D_MODEL, D_MODEL), lambda i, j: (i, 0, j)),
            pl.BlockSpec((None, 1, D_MODEL), lambda i, j: (i, 0, j)),
        ],
        out_specs=pl.BlockSpec((None, batch, D_MODEL), lambda i, j: (i, 0, j)),
        out_shape=jax.ShapeDtypeStruct((DEPTH, batch, 3 * D_MODEL), F32),
        compiler_params=pltpu.CompilerParams(
            dimension_semantics=("arbitrary", "arbitrary"), vmem_limit_bytes=VMEM_LIMIT_BYTES),
        name="adaln_mod",
    )(c, ada_w, ada_b.reshape(DEPTH, 1, 3 * D_MODEL))


def _gla_kernel(x_ref, mod_ref, ng_ref, wq_ref, wk_ref, wv_ref, wr_ref, wg_ref, wa_ref, ba_ref,
                hng_ref, wo_ref, out_ref, state_ref, q_scr, k_scr, v_scr, b_scr, o_scr):
    ts = GLA_ROW_TILE
    ch = GLA_CHUNK

    @pl.when(pl.program_id(1) == 0)
    def _():
        state_ref[...] = jnp.zeros_like(state_ref)

    x = x_ref[...]
    shift, scale, gate = mod_ref[0:1, :], mod_ref[1:2, :], mod_ref[2:3, :]
    hb = _modulated_rmsnorm(x, ng_ref[...], shift, scale).astype(BF16)

    q_scr[...] = _dot(hb, wq_ref[...]) * (GLA_DK_HEAD ** -0.5)
    k_scr[...] = _dot(hb, wk_ref[...])
    v_scr[...] = _dot(hb, wv_ref[...])
    g_lr = _dot(hb, wg_ref[...])
    z = _dot(g_lr.astype(BF16), wa_ref[...]) + ba_ref[...]
    b_scr[...] = _chunk_cumsum(_log_sigmoid(z) / GLA_TAU, ch)

    ci = lax.broadcasted_iota(jnp.int32, (ch, ch), 0)
    cj = lax.broadcasted_iota(jnp.int32, (ch, ch), 1)
    causal = cj <= ci
    n_chunks = ts // ch
    heads = range(GLA_HEADS)

    def k_lanes(hd):
        return pl.ds(hd * GLA_DK_HEAD, GLA_DK_HEAD)

    def v_lanes(hd):
        return pl.ds(hd * GLA_DV_HEAD, GLA_DV_HEAD)

    def intra_scores(c):
        rows = pl.ds(c * ch, ch)
        out = []
        for hd in heads:
            bc = b_scr[rows, k_lanes(hd)]
            mid = bc[ch // 2:ch // 2 + 1, :]
            qe = (q_scr[rows, k_lanes(hd)] * jnp.exp(bc - mid)).astype(BF16)
            ke = (k_scr[rows, k_lanes(hd)] * jnp.exp(mid - bc)).astype(BF16)
            out.append(_dot_nt(qe, ke))
        return out

    a_next = intra_scores(0)
    for c in range(n_chunks):
        rows = pl.ds(c * ch, ch)
        a_cur = a_next
        if c + 1 < n_chunks:
            a_next = intra_scores(c + 1)
        for hd in heads:
            bc = b_scr[rows, k_lanes(hd)]
            last = bc[ch - 1:ch, :]
            qc = q_scr[rows, k_lanes(hd)]
            kc = k_scr[rows, k_lanes(hd)]
            vcb = v_scr[rows, v_lanes(hd)].astype(BF16)
            st = state_ref[hd]
            kdt = (kc * jnp.exp(last - bc)).T.astype(BF16)
            a = jnp.where(causal, a_cur[hd], 0.0).astype(BF16)
            res = _dot(jnp.concatenate([a, kdt], axis=0), vcb)
            inter = _dot((qc * jnp.exp(bc)).astype(BF16), st.astype(BF16))
            decay = jnp.broadcast_to(jnp.exp(last), (GLA_DK_HEAD, GLA_DK_HEAD)).T
            decay = jnp.concatenate([decay] * (GLA_DV_HEAD // GLA_DK_HEAD), axis=1)
            state_ref[hd] = st * decay + res[ch:]
            o_scr[rows, v_lanes(hd)] = res[:ch] + inter

    r = _dot(hb, wr_ref[...])
    parts = []
    for hd in range(GLA_HEADS):
        oh = o_scr[:, pl.ds(hd * GLA_DV_HEAD, GLA_DV_HEAD)]
        yh = oh * lax.rsqrt(jnp.mean(oh * oh, axis=-1, keepdims=True) + NORM_EPS)
        parts.append(yh * hng_ref[...])
    og = (jnp.concatenate(parts, axis=-1) * _silu(r)).astype(BF16)
    out_ref[...] = x + gate * _dot(og, wo_ref[...])


def _gla_layer(x, mod, norm_g, w_in, w_alpha, b_alpha, head_norm_g, w_out):
    batch, seq, _ = x.shape
    ts = GLA_ROW_TILE
    o_q, o_k, o_v, o_g, o_r = 0, GLA_DK, 2 * GLA_DK, 2 * GLA_DK + GLA_DV, 2 * GLA_DK + GLA_DV + GLA_RANK
    w_in_b = w_in.astype(BF16)
    wq, wk, wv = w_in_b[:, o_q:o_k], w_in_b[:, o_k:o_v], w_in_b[:, o_v:o_g]
    wr = w_in_b[:, o_r:]
    wg = jnp.pad(w_in_b[:, o_g:o_r], ((0, 0), (0, LANES - GLA_RANK)))
    wa = jnp.pad(w_alpha, ((0, LANES - GLA_RANK), (0, 0))).astype(BF16)

    def const(shape):
        return pl.BlockSpec(shape, lambda b, t: (0,) * len(shape))

    return pl.pallas_call(
        _gla_kernel,
        grid=(batch, seq // ts),
        in_specs=[
            pl.BlockSpec((None, ts, D_MODEL), lambda b, t: (b, t, 0)),
            pl.BlockSpec((None, 3, D_MODEL), lambda b, t: (b, 0, 0)),
            const((1, D_MODEL)),
            const((D_MODEL, GLA_DK)), const((D_MODEL, GLA_DK)), const((D_MODEL, GLA_DV)),
            const((D_MODEL, GLA_DV)), const((D_MODEL, LANES)),
            const((LANES, GLA_DK)), const((1, GLA_DK)),
            const((1, GLA_DV_HEAD)), const((GLA_DV, D_MODEL)),
        ],
        out_specs=pl.BlockSpec((None, ts, D_MODEL), lambda b, t: (b, t, 0)),
        out_shape=jax.ShapeDtypeStruct(x.shape, F32),
        scratch_shapes=[
            pltpu.VMEM((GLA_HEADS, GLA_DK_HEAD, GLA_DV_HEAD), F32),
            pltpu.VMEM((ts, GLA_DK), F32), pltpu.VMEM((ts, GLA_DK), F32),
            pltpu.VMEM((ts, GLA_DV), F32), pltpu.VMEM((ts, GLA_DK), F32),
            pltpu.VMEM((ts, GLA_DV), F32),
        ],
        compiler_params=pltpu.CompilerParams(
            dimension_semantics=("arbitrary", "arbitrary"), vmem_limit_bytes=VMEM_LIMIT_BYTES),
        name="gla_layer",
    )(x, mod, norm_g.reshape(1, D_MODEL), wq, wk, wv, wr, wg, wa, b_alpha.reshape(1, GLA_DK),
      head_norm_g.reshape(1, GLA_DV_HEAD), w_out.astype(BF16))


def _section_group(j):
    return jnp.where(j == 3 * N_GROUPS, 0, j // 3)


def _dsw_in_proj_kernel(x_ref, mod_ref, ng_ref, w_ref, out_ref, h_scr, hf_scr):
    seq = x_ref.shape[0]
    rc = PROJ_ROW_CHUNK
    n_lane_blocks = D_MODEL // LANES
    j = pl.program_id(1)

    @pl.when(j == 0)
    def _():
        shift, scale = mod_ref[0:1, :], mod_ref[1:2, :]
        g = ng_ref[...]
        for s in range(seq // rc):
            rows = pl.ds(s * rc, rc)
            h = _modulated_rmsnorm(x_ref[rows, :], g, shift, scale)
            h_scr[0, rows, :] = h.astype(BF16)
            for cb in range(n_lane_blocks):
                hf_scr[cb, rows, :] = h[:, cb * LANES:(cb + 1) * LANES]
        for gi, (_, dil) in enumerate(DSW_GROUPS):
            if dil == 1:
                continue
            seg = seq // dil
            n = min(rc, seg)
            for r in range(dil):
                for s in range(seg // n):
                    for cb in range(n_lane_blocks):
                        piece = hf_scr[cb, pl.ds(r + s * n * dil, n, stride=dil), :]
                        h_scr[gi, pl.ds(r * seg + s * n, n), pl.ds(cb * LANES, LANES)] = piece.astype(BF16)

    gi = _section_group(j)
    w = w_ref[...]
    is_query = (j % 3 == 0) & (j < 3 * N_GROUPS)
    out_scale = jnp.where(is_query, QUERY_SCALE, 1.0).astype(F32)
    for s in range(seq // rc):
        rows = pl.ds(s * rc, rc)
        out_ref[rows, :] = (_dot(h_scr[gi, rows, :], w) * out_scale).astype(BF16)


def _dsw_in_proj(x, mod, norm_g, w_in):
    batch, seq, _ = x.shape
    n_sec = 3 * N_GROUPS + 1
    return pl.pallas_call(
        _dsw_in_proj_kernel,
        grid=(batch, n_sec),
        in_specs=[
            pl.BlockSpec((None, seq, D_MODEL), lambda b, j: (b, 0, 0)),
            pl.BlockSpec((None, 3, D_MODEL), lambda b, j: (b, 0, 0)),
            pl.BlockSpec((1, D_MODEL), lambda b, j: (0, 0)),
            pl.BlockSpec((D_MODEL, DSW_WIDTH), lambda b, j: (0, j)),
        ],
        out_specs=pl.BlockSpec((None, None, seq, DSW_WIDTH), lambda b, j: (b, j, 0, 0)),
        out_shape=jax.ShapeDtypeStruct((batch, n_sec, seq, DSW_WIDTH), BF16),
        scratch_shapes=[pltpu.VMEM((N_GROUPS, seq, D_MODEL), BF16),
                        pltpu.VMEM((D_MODEL // LANES, seq, LANES), F32)],
        compiler_params=pltpu.CompilerParams(
            dimension_semantics=("arbitrary", "arbitrary"), vmem_limit_bytes=VMEM_LIMIT_BYTES),
        name="dsw_in_proj",
    )(x, mod, norm_g.reshape(1, D_MODEL), w_in.astype(BF16))


def _t5_causal_bucket(n):
    max_exact = REL_BUCKETS // 2
    nf = np.maximum(n, 1).astype(np.float32)
    large = max_exact + (np.log(nf / max_exact) / math.log(REL_MAX_DIST / max_exact)
                         * (REL_BUCKETS - max_exact)).astype(np.int32)
    large = np.minimum(large, REL_BUCKETS - 1)
    return np.where(n < max_exact, n, large).astype(np.int32)


def _bucket_tables():
    qi = np.arange(DSW_SPAN)[:, None]
    kj = np.arange(2 * DSW_SPAN)[None, :]
    steps = qi + DSW_SPAN - kj
    in_window = (steps >= 0) & (steps <= DSW_SPAN)
    tables = []
    for _, dil in DSW_GROUPS:
        bucket = _t5_causal_bucket(np.clip(steps, 0, DSW_SPAN) * dil)
        tables.append(np.where(in_window, bucket, -1))
    return np.stack(tables).astype(np.int32)


def _dsw_bias_kernel(rel_ref, bucket_ref, out_ref):
    bucket = bucket_ref[...]
    for h in range(DSW_HEADS):
        acc = jnp.where(bucket < 0, -jnp.inf, 0.0).astype(F32)
        for bk in range(REL_BUCKETS):
            acc = jnp.where(bucket == bk, rel_ref[bk, h], acc)
        out_ref[pl.ds(h * DSW_SPAN, DSW_SPAN), :] = acc * LOG2E


def _dsw_bias(rel_bias):
    return pl.pallas_call(
        _dsw_bias_kernel,
        grid=(N_GROUPS,),
        in_specs=[
            pl.BlockSpec(memory_space=pltpu.SMEM),
            pl.BlockSpec((None, DSW_SPAN, 2 * DSW_SPAN), lambda g: (g, 0, 0)),
        ],
        out_specs=pl.BlockSpec((None, DSW_HEADS * DSW_SPAN, 2 * DSW_SPAN), lambda g: (g, 0, 0)),
        out_shape=jax.ShapeDtypeStruct((N_GROUPS, DSW_HEADS * DSW_SPAN, 2 * DSW_SPAN), F32),
        compiler_params=pltpu.CompilerParams(dimension_semantics=("arbitrary",)),
        name="dsw_bias",
    )(rel_bias, jnp.asarray(_bucket_tables()))


def _dsw_attention_kernel(*refs):
    qkv_refs = refs[:3 * N_GROUPS]
    gate_ref, bias_ref, out_ref, o_scr, lse_scr = refs[3 * N_GROUPS:]
    seq = out_ref.shape[0]
    span = DSW_SPAN
    lane = lax.broadcasted_iota(jnp.int32, (span, LANES), 1)
    head_masks = [(lane // DSW_HEAD_DIM) == hh for hh in range(HEADS_PER_STEP)]

    units = [(gi, blk) for gi in range(N_GROUPS) for blk in range(seq // span)]

    def is_first(gi, blk):
        return blk % (seq // DSW_GROUPS[gi][1] // span) == 0

    def key_rows(gi, blk):
        return pl.ds(blk * span, span) if is_first(gi, blk) else pl.ds((blk - 1) * span, 2 * span)

    def scores(gi, blk):
        q = qkv_refs[3 * gi][pl.ds(blk * span, span), :]
        q2 = jnp.concatenate([jnp.where(mk, q, jnp.zeros_like(q)) for mk in head_masks], axis=0)
        return _dot_nt(q2, qkv_refs[3 * gi + 1][key_rows(gi, blk), :])

    s_next = scores(*units[0])
    for idx, (gi, blk) in enumerate(units):
        dil = DSW_GROUPS[gi][1]
        s = s_next
        if idx + 1 < len(units):
            s_next = scores(*units[idx + 1])
        if is_first(gi, blk):
            s = s + bias_ref[gi, :, pl.ds(span, span)]
        else:
            s = s + bias_ref[gi]
        m = jnp.max(s, axis=-1, keepdims=True)
        e = jnp.exp2(s - m)
        l = jnp.sum(e, axis=-1, keepdims=True)
        pv = _dot(e.astype(BF16), qkv_refs[3 * gi + 2][key_rows(gi, blk), :]) / l
        lse2 = m + jnp.log2(l)
        o_blk = jnp.where(head_masks[0], pv[:span], pv[span:])
        lse_blk = jnp.where(head_masks[0], lse2[:span], lse2[span:])
        blocks_per_class = seq // dil // span
        r = blk // blocks_per_class
        i0 = (blk % blocks_per_class) * span
        dst = pl.ds(i0 * dil + r, span, stride=dil) if dil > 1 else pl.ds(blk * span, span)
        o_scr[gi, dst, :] = o_blk
        lse_scr[gi, dst, :] = lse_blk

    lse = [lse_scr[gi] for gi in range(N_GROUPS)]
    top = functools.reduce(jnp.maximum, lse)
    wts = [jnp.exp2(v - top) for v in lse]
    total = functools.reduce(lambda a, b: a + b, wts)
    acc = functools.reduce(lambda a, b: a + b, [wts[gi] * o_scr[gi] for gi in range(N_GROUPS)])
    out_ref[...] = ((acc / total) * _silu(gate_ref[...].astype(F32))).astype(BF16)


def _dsw_attention(proj, bias):
    batch, n_sec, seq, _ = proj.shape
    n_steps = DSW_HEADS // HEADS_PER_STEP

    def sec(j):
        return pl.BlockSpec((None, None, seq, LANES), lambda b, hp, j=j: (b, j, 0, hp))

    return pl.pallas_call(
        _dsw_attention_kernel,
        grid=(batch, n_steps),
        in_specs=[sec(j) for j in range(n_sec)] + [
            pl.BlockSpec((N_GROUPS, HEADS_PER_STEP * DSW_SPAN, 2 * DSW_SPAN), lambda b, hp: (0, hp, 0)),
        ],
        out_specs=pl.BlockSpec((None, seq, LANES), lambda b, hp: (b, 0, hp)),
        out_shape=jax.ShapeDtypeStruct((batch, seq, DSW_WIDTH), BF16),
        scratch_shapes=[pltpu.VMEM((N_GROUPS, seq, LANES), F32), pltpu.VMEM((N_GROUPS, seq, LANES), F32)],
        compiler_params=pltpu.CompilerParams(
            dimension_semantics=("arbitrary", "arbitrary"), vmem_limit_bytes=VMEM_LIMIT_BYTES),
        name="dsw_attention",
    )(*([proj] * n_sec), bias)


def _dsw_out_proj_kernel(o_ref, x_ref, mod_ref, w_ref, fg_ref, out_ref):
    gate = mod_ref[2:3, :]
    xn = x_ref[...] + gate * _dot(o_ref[...], w_ref[...])
    out_ref[...] = xn * lax.rsqrt(jnp.mean(xn * xn, axis=-1, keepdims=True) + NORM_EPS) * fg_ref[...]


def _dsw_out_proj(og, x, mod, w_out, final_g):
    batch, seq, _ = x.shape
    ts = OUT_ROW_TILE
    return pl.pallas_call(
        _dsw_out_proj_kernel,
        grid=(batch, seq // ts),
        in_specs=[
            pl.BlockSpec((None, ts, DSW_WIDTH), lambda b, t: (b, t, 0)),
            pl.BlockSpec((None, ts, D_MODEL), lambda b, t: (b, t, 0)),
            pl.BlockSpec((None, 3, D_MODEL), lambda b, t: (b, 0, 0)),
            pl.BlockSpec((DSW_WIDTH, D_MODEL), lambda b, t: (0, 0)),
            pl.BlockSpec((1, D_MODEL), lambda b, t: (0, 0)),
        ],
        out_specs=pl.BlockSpec((None, ts, D_MODEL), lambda b, t: (b, t, 0)),
        out_shape=jax.ShapeDtypeStruct(x.shape, F32),
        compiler_params=pltpu.CompilerParams(
            dimension_semantics=("arbitrary", "arbitrary"), vmem_limit_bytes=VMEM_LIMIT_BYTES),
        name="dsw_out_proj",
    )(og, x, mod, w_out.astype(BF16), final_g.reshape(1, D_MODEL))


def kernel(x, c, ada_w, ada_b, norm_g, gla_w_in, gla_w_alpha, gla_b_alpha, gla_norm_g, gla_w_out,
           dsw_w_in, dsw_w_out, rel_bias, final_g):
    batch = x.shape[0]
    mod = _adaln_mod(c, ada_w, ada_b)
    mod = mod.reshape(DEPTH, batch, 3, D_MODEL)
    x = _gla_layer(x, mod[0], norm_g[0], gla_w_in[0], gla_w_alpha[0], gla_b_alpha[0], gla_norm_g[0],
                   gla_w_out[0])
    proj = _dsw_in_proj(x, mod[1], norm_g[1], dsw_w_in[0])
    bias = _dsw_bias(rel_bias)
    og = _dsw_attention(proj, bias)
    return _dsw_out_proj(og, x, mod[1], dsw_w_out[0], final_g)
```

```python
import functools
import math

import jax
import jax.numpy as jnp
import numpy as np
from jax import lax
from jax.experimental import pallas as pl
from jax.experimental.pallas import tpu as pltpu

F32 = jnp.float32
BF16 = jnp.bfloat16

D_MODEL = 1024
DEPTH = 2
NORM_EPS = 1e-6

GLA_HEADS = 4
GLA_DK = D_MODEL // 2
GLA_DV = D_MODEL
GLA_DK_HEAD = GLA_DK // GLA_HEADS
GLA_DV_HEAD = GLA_DV // GLA_HEADS
GLA_RANK = 16
GLA_TAU = 16.0
GLA_CHUNK = 64

DSW_HEADS = 16
DSW_HEAD_DIM = D_MODEL // DSW_HEADS
DSW_WIDTH = DSW_HEADS * DSW_HEAD_DIM
DSW_GROUPS = ((128, 1), (512, 4), (2048, 16))
N_GROUPS = len(DSW_GROUPS)
DSW_SPAN = 128
REGATHER_STRIDE = 4
assert N_GROUPS == 3 and all(b[1] == a[1] * REGATHER_STRIDE for a, b in zip(DSW_GROUPS, DSW_GROUPS[1:]))


def _segment_residues():
    table = [[0]]
    for gi in range(1, N_GROUPS):
        prev_dil = DSW_GROUPS[gi - 1][1]
        table.append([res + prev_dil * r2 for res in table[-1] for r2 in range(REGATHER_STRIDE)])
    return table


SEG_RESIDUE = _segment_residues()
REL_BUCKETS = 32
REL_MAX_DIST = 2048

LANES = 128
HEADS_PER_STEP = LANES // DSW_HEAD_DIM
assert HEADS_PER_STEP == 2
LOG2E = math.log2(math.e)
QUERY_SCALE = DSW_HEAD_DIM ** -0.5 * LOG2E
VMEM_LIMIT_BYTES = 56 * 1024 * 1024

GLA_ROW_TILE = 256
OUT_ROW_TILE = 512
PROJ_ROW_CHUNK = 256

_NT = (((1,), (1,)), ((), ()))


def _dot(a, b, precision=None):
    return jnp.dot(a, b, preferred_element_type=F32, precision=precision)


def _dot_nt(a, b):
    return lax.dot_general(a, b, _NT, preferred_element_type=F32)


def _silu(v):
    return v * (1.0 / (1.0 + jnp.exp(-v)))


def _log_sigmoid(z):
    return jnp.minimum(z, 0.0) - jnp.log(1.0 + jnp.exp(-jnp.abs(z)))


def _chunk_cumsum(x, chunk):
    pos = lax.broadcasted_iota(jnp.int32, x.shape, 0) % chunk
    step = 1
    while step < chunk:
        x = x + jnp.where(pos >= step, pltpu.roll(x, shift=step, axis=0), 0.0)
        step *= 2
    return x


def _modulated_rmsnorm(x, g, shift, scale):
    y = x * lax.rsqrt(jnp.mean(x * x, axis=-1, keepdims=True) + NORM_EPS)
    return y * g * (1.0 + scale) + shift


def _adaln_kernel(c_ref, w_ref, b_ref, out_ref):
    c_act = _silu(c_ref[...])
    out_ref[...] = _dot(c_act, w_ref[...], precision=lax.Precision.HIGHEST) + b_ref[...]


def _adaln_mod(c, ada_w, ada_b):
    batch = c.shape[0]
    n_col = 3
    return pl.pallas_call(
        _adaln_kernel,
        grid=(DEPTH, n_col),
        in_specs=[
            pl.BlockSpec((batch, D_MODEL), lambda i, j: (0, 0)),
            pl.BlockSpec((None, D_MODEL, D_MODEL), lambda i, j: (i, 0, j)),
            pl.BlockSpec((None, 1, D_MODEL), lambda i, j: (i, 0, j)),
        ],
        out_specs=pl.BlockSpec((None, batch, D_MODEL), lambda i, j: (i, 0, j)),
        out_shape=jax.ShapeDtypeStruct((DEPTH, batch, 3 * D_MODEL), F32),
        compiler_params=pltpu.CompilerParams(
            dimension_semantics=("arbitrary", "arbitrary"), vmem_limit_bytes=VMEM_LIMIT_BYTES),
        name="adaln_mod",
    )(c, ada_w, ada_b.reshape(DEPTH, 1, 3 * D_MODEL))


def _gla_kernel(x_ref, mod_ref, ng_ref, wq_ref, wk_ref, wv_ref, wr_ref, wg_ref, wa_ref, ba_ref,
                hng_ref, wo_ref, out_ref, state_ref, q_scr, k_scr, v_scr, b_scr, o_scr):
    ts = GLA_ROW_TILE
    ch = GLA_CHUNK

    @pl.when(pl.program_id(1) == 0)
    def _():
        state_ref[...] = jnp.zeros_like(state_ref)

    x = x_ref[...]
    shift, scale, gate = mod_ref[0:1, :], mod_ref[1:2, :], mod_ref[2:3, :]
    hb = _modulated_rmsnorm(x, ng_ref[...], shift, scale).astype(BF16)

    q_scr[...] = _dot(hb, wq_ref[...]) * (GLA_DK_HEAD ** -0.5)
    k_scr[...] = _dot(hb, wk_ref[...])
    v_scr[...] = _dot(hb, wv_ref[...])
    g_lr = _dot(hb, wg_ref[...])
    z = _dot(g_lr.astype(BF16), wa_ref[...]) + ba_ref[...]
    b_scr[...] = _chunk_cumsum(_log_sigmoid(z) / GLA_TAU, ch)

    ci = lax.broadcasted_iota(jnp.int32, (ch, ch), 0)
    cj = lax.broadcasted_iota(jnp.int32, (ch, ch), 1)
    causal = cj <= ci
    n_chunks = ts // ch
    heads = range(GLA_HEADS)

    def k_lanes(hd):
        return pl.ds(hd * GLA_DK_HEAD, GLA_DK_HEAD)

    def v_lanes(hd):
        return pl.ds(hd * GLA_DV_HEAD, GLA_DV_HEAD)

    def intra_scores(c):
        rows = pl.ds(c * ch, ch)
        out = []
        for hd in heads:
            bc = b_scr[rows, k_lanes(hd)]
            mid = bc[ch // 2:ch // 2 + 1, :]
            qe = (q_scr[rows, k_lanes(hd)] * jnp.exp(bc - mid)).astype(BF16)
            ke = (k_scr[rows, k_lanes(hd)] * jnp.exp(mid - bc)).astype(BF16)
            out.append(_dot_nt(qe, ke))
        return out

    a_next = intra_scores(0)
    for c in range(n_chunks):
        rows = pl.ds(c * ch, ch)
        a_cur = a_next
        if c + 1 < n_chunks:
            a_next = intra_scores(c + 1)
        for hd in heads:
            bc = b_scr[rows, k_lanes(hd)]
            last = bc[ch - 1:ch, :]
            qc = q_scr[rows, k_lanes(hd)]
            kc = k_scr[rows, k_lanes(hd)]
            vcb = v_scr[rows, v_lanes(hd)].astype(BF16)
            st = state_ref[hd]
            kdt = (kc * jnp.exp(last - bc)).T.astype(BF16)
            a = jnp.where(causal, a_cur[hd], 0.0).astype(BF16)
            res = _dot(jnp.concatenate([a, kdt], axis=0), vcb)
            inter = _dot((qc * jnp.exp(bc)).astype(BF16), st.astype(BF16))
            decay = jnp.broadcast_to(jnp.exp(last), (GLA_DK_HEAD, GLA_DK_HEAD)).T
            decay = jnp.concatenate([decay] * (GLA_DV_HEAD // GLA_DK_HEAD), axis=1)
            state_ref[hd] = st * decay + res[ch:]
            o_scr[rows, v_lanes(hd)] = res[:ch] + inter

    r = _dot(hb, wr_ref[...])
    parts = []
    for hd in range(GLA_HEADS):
        oh = o_scr[:, pl.ds(hd * GLA_DV_HEAD, GLA_DV_HEAD)]
        yh = oh * lax.rsqrt(jnp.mean(oh * oh, axis=-1, keepdims=True) + NORM_EPS)
        parts.append(yh * hng_ref[...])
    og = (jnp.concatenate(parts, axis=-1) * _silu(r)).astype(BF16)
    out_ref[...] = x + gate * _dot(og, wo_ref[...])


def _gla_layer(x, mod, norm_g, w_in, w_alpha, b_alpha, head_norm_g, w_out):
    batch, seq, _ = x.shape
    ts = GLA_ROW_TILE
    o_q, o_k, o_v, o_g, o_r = 0, GLA_DK, 2 * GLA_DK, 2 * GLA_DK + GLA_DV, 2 * GLA_DK + GLA_DV + GLA_RANK
    w_in_b = w_in.astype(BF16)
    wq, wk, wv = w_in_b[:, o_q:o_k], w_in_b[:, o_k:o_v], w_in_b[:, o_v:o_g]
    wr = w_in_b[:, o_r:]
    wg = jnp.pad(w_in_b[:, o_g:o_r], ((0, 0), (0, LANES - GLA_RANK)))
    wa = jnp.pad(w_alpha, ((0, LANES - GLA_RANK), (0, 0))).astype(BF16)

    def const(shape):
        return pl.BlockSpec(shape, lambda b, t: (0,) * len(shape))

    return pl.pallas_call(
        _gla_kernel,
        grid=(batch, seq // ts),
        in_specs=[
            pl.BlockSpec((None, ts, D_MODEL), lambda b, t: (b, t, 0)),
            pl.BlockSpec((None, 3, D_MODEL), lambda b, t: (b, 0, 0)),
            const((1, D_MODEL)),
            const((D_MODEL, GLA_DK)), const((D_MODEL, GLA_DK)), const((D_MODEL, GLA_DV)),
            const((D_MODEL, GLA_DV)), const((D_MODEL, LANES)),
            const((LANES, GLA_DK)), const((1, GLA_DK)),
            const((1, GLA_DV_HEAD)), const((GLA_DV, D_MODEL)),
        ],
        out_specs=pl.BlockSpec((None, ts, D_MODEL), lambda b, t: (b, t, 0)),
        out_shape=jax.ShapeDtypeStruct(x.shape, F32),
        scratch_shapes=[
            pltpu.VMEM((GLA_HEADS, GLA_DK_HEAD, GLA_DV_HEAD), F32),
            pltpu.VMEM((ts, GLA_DK), F32), pltpu.VMEM((ts, GLA_DK), F32),
            pltpu.VMEM((ts, GLA_DV), F32), pltpu.VMEM((ts, GLA_DK), F32),
            pltpu.VMEM((ts, GLA_DV), F32),
        ],
        compiler_params=pltpu.CompilerParams(
            dimension_semantics=("arbitrary", "arbitrary"), vmem_limit_bytes=VMEM_LIMIT_BYTES),
        name="gla_layer",
    )(x, mod, norm_g.reshape(1, D_MODEL), wq, wk, wv, wr, wg, wa, b_alpha.reshape(1, GLA_DK),
      head_norm_g.reshape(1, GLA_DV_HEAD), w_out.astype(BF16))


def _section_group(j):
    return jnp.where(j == 3 * N_GROUPS, 0, j // 3)


def _dsw_in_proj_kernel(x_ref, mod_ref, ng_ref, w_ref, out_ref, h_scr, hf_scr, tmp_scr):
    seq = x_ref.shape[0]
    rc = PROJ_ROW_CHUNK
    n_lane_blocks = D_MODEL // LANES
    j = pl.program_id(1)

    @pl.when(j == 0)
    def _():
        shift, scale = mod_ref[0:1, :], mod_ref[1:2, :]
        g = ng_ref[...]
        for s in range(seq // rc):
            rows = pl.ds(s * rc, rc)
            h = _modulated_rmsnorm(x_ref[rows, :], g, shift, scale)
            h_scr[0, rows, :] = h.astype(BF16)
            for cb in range(n_lane_blocks):
                hf_scr[cb, rows, :] = h[:, cb * LANES:(cb + 1) * LANES]
        for cb in range(n_lane_blocks):
            lanes = pl.ds(cb * LANES, LANES)
            src = hf_scr.at[cb]
            for gi in range(1, N_GROUPS):
                prev_len = seq // DSW_GROUPS[gi - 1][1]
                seg_len = prev_len // REGATHER_STRIDE
                n = min(rc, seg_len)
                for p in range(DSW_GROUPS[gi - 1][1]):
                    for r2 in range(REGATHER_STRIDE):
                        for s in range(seg_len // n):
                            piece = src[pl.ds(p * prev_len + r2 + s * n * REGATHER_STRIDE, n,
                                              stride=REGATHER_STRIDE), :]
                            dst = pl.ds((p * REGATHER_STRIDE + r2) * seg_len + s * n, n)
                            h_scr[gi, dst, lanes] = piece.astype(BF16)
                            if gi + 1 < N_GROUPS:
                                tmp_scr[dst, :] = piece
                src = tmp_scr

    gi = _section_group(j)
    w = w_ref[...]
    is_query = (j % 3 == 0) & (j < 3 * N_GROUPS)
    out_scale = jnp.where(is_query, QUERY_SCALE, 1.0).astype(F32)
    for s in range(seq // rc):
        rows = pl.ds(s * rc, rc)
        out_ref[rows, :] = (_dot(h_scr[gi, rows, :], w) * out_scale).astype(BF16)


def _dsw_in_proj(x, mod, norm_g, w_in):
    batch, seq, _ = x.shape
    n_sec = 3 * N_GROUPS + 1
    return pl.pallas_call(
        _dsw_in_proj_kernel,
        grid=(batch, n_sec),
        in_specs=[
            pl.BlockSpec((None, seq, D_MODEL), lambda b, j: (b, 0, 0)),
            pl.BlockSpec((None, 3, D_MODEL), lambda b, j: (b, 0, 0)),
            pl.BlockSpec((1, D_MODEL), lambda b, j: (0, 0)),
            pl.BlockSpec((D_MODEL, DSW_WIDTH), lambda b, j: (0, j)),
        ],
        out_specs=pl.BlockSpec((None, None, seq, DSW_WIDTH), lambda b, j: (b, j, 0, 0)),
        out_shape=jax.ShapeDtypeStruct((batch, n_sec, seq, DSW_WIDTH), BF16),
        scratch_shapes=[pltpu.VMEM((N_GROUPS, seq, D_MODEL), BF16),
                        pltpu.VMEM((D_MODEL // LANES, seq, LANES), F32),
                        pltpu.VMEM((seq, LANES), F32)],
        compiler_params=pltpu.CompilerParams(
            dimension_semantics=("arbitrary", "arbitrary"), vmem_limit_bytes=VMEM_LIMIT_BYTES),
        name="dsw_in_proj",
    )(x, mod, norm_g.reshape(1, D_MODEL), w_in.astype(BF16))


def _t5_causal_bucket(n):
    max_exact = REL_BUCKETS // 2
    nf = np.maximum(n, 1).astype(np.float32)
    large = max_exact + (np.log(nf / max_exact) / math.log(REL_MAX_DIST / max_exact)
                         * (REL_BUCKETS - max_exact)).astype(np.int32)
    large = np.minimum(large, REL_BUCKETS - 1)
    return np.where(n < max_exact, n, large).astype(np.int32)


def _bucket_tables():
    qi = np.arange(DSW_SPAN)[:, None]
    kj = np.arange(2 * DSW_SPAN)[None, :]
    steps = qi + DSW_SPAN - kj
    in_window = (steps >= 0) & (steps <= DSW_SPAN)
    tables = []
    for _, dil in DSW_GROUPS:
        bucket = _t5_causal_bucket(np.clip(steps, 0, DSW_SPAN) * dil)
        tables.append(np.where(in_window, bucket, -1))
    return np.stack(tables).astype(np.int32)


def _dsw_bias_kernel(rel_ref, bucket_ref, out_ref):
    bucket = bucket_ref[...]
    for h in range(DSW_HEADS):
        acc = jnp.where(bucket < 0, -jnp.inf, 0.0).astype(F32)
        for bk in range(REL_BUCKETS):
            acc = jnp.where(bucket == bk, rel_ref[bk, h], acc)
        out_ref[pl.ds(h * DSW_SPAN, DSW_SPAN), :] = acc * LOG2E


def _dsw_bias(rel_bias):
    return pl.pallas_call(
        _dsw_bias_kernel,
        grid=(N_GROUPS,),
        in_specs=[
            pl.BlockSpec(memory_space=pltpu.SMEM),
            pl.BlockSpec((None, DSW_SPAN, 2 * DSW_SPAN), lambda g: (g, 0, 0)),
        ],
        out_specs=pl.BlockSpec((None, DSW_HEADS * DSW_SPAN, 2 * DSW_SPAN), lambda g: (g, 0, 0)),
        out_shape=jax.ShapeDtypeStruct((N_GROUPS, DSW_HEADS * DSW_SPAN, 2 * DSW_SPAN), F32),
        compiler_params=pltpu.CompilerParams(dimension_semantics=("arbitrary",)),
        name="dsw_bias",
    )(rel_bias, jnp.asarray(_bucket_tables()))


def _dsw_attention_kernel(*refs):
    qkv_refs = refs[:3 * N_GROUPS]
    gate_ref, bias_ref, out_ref, pv_scr, m_scr, l_scr = refs[3 * N_GROUPS:]
    seq = out_ref.shape[0]
    span = DSW_SPAN
    lane = lax.broadcasted_iota(jnp.int32, (span, LANES), 1)
    head_masks = [(lane // DSW_HEAD_DIM) == hh for hh in range(HEADS_PER_STEP)]

    units = [(gi, blk) for gi in range(N_GROUPS) for blk in range(seq // span)]

    def is_first(gi, blk):
        return blk % (seq // DSW_GROUPS[gi][1] // span) == 0

    def key_rows(gi, blk):
        return pl.ds(blk * span, span) if is_first(gi, blk) else pl.ds((blk - 1) * span, 2 * span)

    def scores(gi, blk):
        q = qkv_refs[3 * gi][pl.ds(blk * span, span), :]
        q2 = jnp.concatenate([jnp.where(mk, q, jnp.zeros_like(q)) for mk in head_masks], axis=0)
        return _dot_nt(q2, qkv_refs[3 * gi + 1][key_rows(gi, blk), :])

    s_next = scores(*units[0])
    for idx, (gi, blk) in enumerate(units):
        dil = DSW_GROUPS[gi][1]
        s = s_next
        if idx + 1 < len(units):
            s_next = scores(*units[idx + 1])
        if is_first(gi, blk):
            s = s + bias_ref[gi, :, pl.ds(span, span)]
        else:
            s = s + bias_ref[gi]
        m = jnp.max(s, axis=-1, keepdims=True)
        e = jnp.exp2(s - m)
        l = jnp.sum(e, axis=-1, keepdims=True)
        pv = _dot(e.astype(BF16), qkv_refs[3 * gi + 2][key_rows(gi, blk), :])
        blocks_per_seg = seq // dil // span
        residue = SEG_RESIDUE[gi][blk // blocks_per_seg]
        i0 = (blk % blocks_per_seg) * span
        dst = pl.ds(i0 * dil + residue, span, stride=dil) if dil > 1 else pl.ds(blk * span, span)
        pv_scr[gi, dst, :] = jnp.where(head_masks[0], pv[:span], pv[span:])
        m_scr[gi, dst, :] = jnp.where(head_masks[0], m[:span], m[span:])
        l_scr[gi, dst, :] = jnp.where(head_masks[0], l[:span], l[span:])

    tops = [m_scr[gi] for gi in range(N_GROUPS)]
    top = functools.reduce(jnp.maximum, tops)
    wts = [jnp.exp2(v - top) for v in tops]
    num = functools.reduce(lambda a, b: a + b, [wts[gi] * pv_scr[gi] for gi in range(N_GROUPS)])
    den = functools.reduce(lambda a, b: a + b, [wts[gi] * l_scr[gi] for gi in range(N_GROUPS)])
    out_ref[...] = ((num / den) * _silu(gate_ref[...].astype(F32))).astype(BF16)


def _dsw_attention(proj, bias):
    batch, n_sec, seq, _ = proj.shape
    n_steps = DSW_HEADS // HEADS_PER_STEP

    def sec(j):
        return pl.BlockSpec((None, None, seq, LANES), lambda b, hp, j=j: (b, j, 0, hp))

    return pl.pallas_call(
        _dsw_attention_kernel,
        grid=(batch, n_steps),
        in_specs=[sec(j) for j in range(n_sec)] + [
            pl.BlockSpec((N_GROUPS, HEADS_PER_STEP * DSW_SPAN, 2 * DSW_SPAN), lambda b, hp: (0, hp, 0)),
        ],
        out_specs=pl.BlockSpec((None, seq, LANES), lambda b, hp: (b, 0, hp)),
        out_shape=jax.ShapeDtypeStruct((batch, seq, DSW_WIDTH), BF16),
        scratch_shapes=[pltpu.VMEM((N_GROUPS, seq, LANES), F32)] * 3,
        compiler_params=pltpu.CompilerParams(
            dimension_semantics=("arbitrary", "arbitrary"), vmem_limit_bytes=VMEM_LIMIT_BYTES),
        name="dsw_attention",
    )(*([proj] * n_sec), bias)


def _dsw_out_proj_kernel(o_ref, x_ref, mod_ref, w_ref, fg_ref, out_ref):
    gate = mod_ref[2:3, :]
    xn = x_ref[...] + gate * _dot(o_ref[...], w_ref[...])
    out_ref[...] = xn * lax.rsqrt(jnp.mean(xn * xn, axis=-1, keepdims=True) + NORM_EPS) * fg_ref[...]


def _dsw_out_proj(og, x, mod, w_out, final_g):
    batch, seq, _ = x.shape
    ts = OUT_ROW_TILE
    return pl.pallas_call(
        _dsw_out_proj_kernel,
        grid=(batch, seq // ts),
        in_specs=[
            pl.BlockSpec((None, ts, DSW_WIDTH), lambda b, t: (b, t, 0)),
            pl.BlockSpec((None, ts, D_MODEL), lambda b, t: (b, t, 0)),
            pl.BlockSpec((None, 3, D_MODEL), lambda b, t: (b, 0, 0)),
            pl.BlockSpec((DSW_WIDTH, D_MODEL), lambda b, t: (0, 0)),
            pl.BlockSpec((1, D_MODEL), lambda b, t: (0, 0)),
        ],
        out_specs=pl.BlockSpec((None, ts, D_MODEL), lambda b, t: (b, t, 0)),
        out_shape=jax.ShapeDtypeStruct(x.shape, F32),
        compiler_params=pltpu.CompilerParams(
            dimension_semantics=("arbitrary", "arbitrary"), vmem_limit_bytes=VMEM_LIMIT_BYTES),
        name="dsw_out_proj",
    )(og, x, mod, w_out.astype(BF16), final_g.reshape(1, D_MODEL))


def kernel(x, c, ada_w, ada_b, norm_g, gla_w_in, gla_w_alpha, gla_b_alpha, gla_norm_g, gla_w_out,
           dsw_w_in, dsw_w_out, rel_bias, final_g):
    batch = x.shape[0]
    mod = _adaln_mod(c, ada_w, ada_b)
    mod = mod.reshape(DEPTH, batch, 3, D_MODEL)
    x = _gla_layer(x, mod[0], norm_g[0], gla_w_in[0], gla_w_alpha[0], gla_b_alpha[0], gla_norm_g[0],
                   gla_w_out[0])
    proj = _dsw_in_proj(x, mod[1], norm_g[1], dsw_w_in[0])
    bias = _dsw_bias(rel_bias)
    og = _dsw_attention(proj, bias)
    return _dsw_out_proj(og, x, mod[1], dsw_w_out[0], final_g)
```

```python
import functools
import math

import jax
import jax.numpy as jnp
import numpy as np
from jax import lax
from jax.experimental import pallas as pl
from jax.experimental.pallas import tpu as pltpu

F32 = jnp.float32
BF16 = jnp.bfloat16

D_MODEL = 1024
DEPTH = 2
NORM_EPS = 1e-6

GLA_HEADS = 4
GLA_DK = D_MODEL // 2
GLA_DV = D_MODEL
GLA_DK_HEAD = GLA_DK // GLA_HEADS
GLA_DV_HEAD = GLA_DV // GLA_HEADS
GLA_RANK = 16
GLA_TAU = 16.0
GLA_CHUNK = 64

DSW_HEADS = 16
DSW_HEAD_DIM = D_MODEL // DSW_HEADS
DSW_WIDTH = DSW_HEADS * DSW_HEAD_DIM
DSW_GROUPS = ((128, 1), (512, 4), (2048, 16))
N_GROUPS = len(DSW_GROUPS)
DSW_SPAN = 128
REGATHER_STRIDE = 4
assert N_GROUPS == 3 and all(b[1] == a[1] * REGATHER_STRIDE for a, b in zip(DSW_GROUPS, DSW_GROUPS[1:]))


def _segment_residues():
    table = [[0]]
    for gi in range(1, N_GROUPS):
        prev_dil = DSW_GROUPS[gi - 1][1]
        table.append([res + prev_dil * r2 for res in table[-1] for r2 in range(REGATHER_STRIDE)])
    return table


SEG_RESIDUE = _segment_residues()
REL_BUCKETS = 32
REL_MAX_DIST = 2048

LANES = 128
HEADS_PER_STEP = LANES // DSW_HEAD_DIM
assert HEADS_PER_STEP == 2
LOG2E = math.log2(math.e)
QUERY_SCALE = DSW_HEAD_DIM ** -0.5 * LOG2E
VMEM_LIMIT_BYTES = 56 * 1024 * 1024

GLA_ROW_TILE = 512
OUT_ROW_TILE = 512
PROJ_ROW_CHUNK = 256

_NT = (((1,), (1,)), ((), ()))


def _dot(a, b, precision=None):
    return jnp.dot(a, b, preferred_element_type=F32, precision=precision)


def _dot_nt(a, b):
    return lax.dot_general(a, b, _NT, preferred_element_type=F32)


def _silu(v):
    return v * (1.0 / (1.0 + jnp.exp(-v)))


def _log_sigmoid(z):
    return jnp.minimum(z, 0.0) - jnp.log(1.0 + jnp.exp(-jnp.abs(z)))


def _chunk_cumsum(x, chunk):
    pos = lax.broadcasted_iota(jnp.int32, x.shape, 0) % chunk
    step = 1
    while step < chunk:
        x = x + jnp.where(pos >= step, pltpu.roll(x, shift=step, axis=0), 0.0)
        step *= 2
    return x


def _modulated_rmsnorm(x, g, shift, scale):
    y = x * lax.rsqrt(jnp.mean(x * x, axis=-1, keepdims=True) + NORM_EPS)
    return y * g * (1.0 + scale) + shift


def _adaln_kernel(c_ref, w_ref, b_ref, out_ref):
    c_act = _silu(c_ref[...])
    out_ref[...] = _dot(c_act, w_ref[...], precision=lax.Precision.HIGHEST) + b_ref[...]


def _adaln_mod(c, ada_w, ada_b):
    batch = c.shape[0]
    n_col = 3
    return pl.pallas_call(
        _adaln_kernel,
        grid=(DEPTH, n_col),
        in_specs=[
            pl.BlockSpec((batch, D_MODEL), lambda i, j: (0, 0)),
            pl.BlockSpec((None, D_MODEL, D_MODEL), lambda i, j: (i, 0, j)),
            pl.BlockSpec((None, 1, D_MODEL), lambda i, j: (i, 0, j)),
        ],
        out_specs=pl.BlockSpec((None, batch, D_MODEL), lambda i, j: (i, 0, j)),
        out_shape=jax.ShapeDtypeStruct((DEPTH, batch, 3 * D_MODEL), F32),
        compiler_params=pltpu.CompilerParams(
            dimension_semantics=("arbitrary", "arbitrary"), vmem_limit_bytes=VMEM_LIMIT_BYTES),
        name="adaln_mod",
    )(c, ada_w, ada_b.reshape(DEPTH, 1, 3 * D_MODEL))


def _gla_kernel(x_ref, mod_ref, ng_ref, wq_ref, wk_ref, wv_ref, wr_ref, wg_ref, wa_ref, ba_ref,
                hng_ref, wo_ref, out_ref, state_ref, q_scr, k_scr, v_scr, b_scr, o_scr, hb_scr):
    ts = GLA_ROW_TILE
    ch = GLA_CHUNK

    @pl.when(pl.program_id(1) == 0)
    def _():
        state_ref[...] = jnp.zeros_like(state_ref)

    shift, scale, gate = mod_ref[0:1, :], mod_ref[1:2, :], mod_ref[2:3, :]
    hb_scr[...] = _modulated_rmsnorm(x_ref[...], ng_ref[...], shift, scale).astype(BF16)

    g_lr = _dot(hb_scr[...], wg_ref[...])
    z = _dot(g_lr.astype(BF16), wa_ref[...]) + ba_ref[...]
    b_scr[...] = _chunk_cumsum(_log_sigmoid(z) / GLA_TAU, ch)
    q_scr[...] = _dot(hb_scr[...], wq_ref[...]) * (GLA_DK_HEAD ** -0.5)
    k_scr[...] = _dot(hb_scr[...], wk_ref[...])
    v_scr[...] = _dot(hb_scr[...], wv_ref[...])

    ci = lax.broadcasted_iota(jnp.int32, (ch, ch), 0)
    cj = lax.broadcasted_iota(jnp.int32, (ch, ch), 1)
    causal = cj <= ci
    n_chunks = ts // ch
    heads = range(GLA_HEADS)

    def k_lanes(hd):
        return pl.ds(hd * GLA_DK_HEAD, GLA_DK_HEAD)

    def v_lanes(hd):
        return pl.ds(hd * GLA_DV_HEAD, GLA_DV_HEAD)

    def intra_scores(c):
        rows = pl.ds(c * ch, ch)
        out = []
        for hd in heads:
            bc = b_scr[rows, k_lanes(hd)]
            mid = bc[ch // 2:ch // 2 + 1, :]
            qe = (q_scr[rows, k_lanes(hd)] * jnp.exp(bc - mid)).astype(BF16)
            ke = (k_scr[rows, k_lanes(hd)] * jnp.exp(mid - bc)).astype(BF16)
            out.append(_dot_nt(qe, ke))
        return out

    a_next = intra_scores(0)
    for c in range(n_chunks):
        rows = pl.ds(c * ch, ch)
        a_cur = a_next
        if c + 1 < n_chunks:
            a_next = intra_scores(c + 1)
        for hd in heads:
            bc = b_scr[rows, k_lanes(hd)]
            last = bc[ch - 1:ch, :]
            qc = q_scr[rows, k_lanes(hd)]
            kc = k_scr[rows, k_lanes(hd)]
            vcb = v_scr[rows, v_lanes(hd)].astype(BF16)
            st = state_ref[hd]
            kdt = (kc * jnp.exp(last - bc)).T.astype(BF16)
            a = jnp.where(causal, a_cur[hd], 0.0).astype(BF16)
            res = _dot(jnp.concatenate([a, kdt], axis=0), vcb)
            inter = _dot((qc * jnp.exp(bc)).astype(BF16), st.astype(BF16))
            decay = jnp.broadcast_to(jnp.exp(last), (GLA_DK_HEAD, GLA_DK_HEAD)).T
            decay = jnp.concatenate([decay] * (GLA_DV_HEAD // GLA_DK_HEAD), axis=1)
            state_ref[hd] = st * decay + res[ch:]
            o_scr[rows, v_lanes(hd)] = res[:ch] + inter

    r = _dot(hb_scr[...], wr_ref[...])
    parts = []
    for hd in range(GLA_HEADS):
        oh = o_scr[:, pl.ds(hd * GLA_DV_HEAD, GLA_DV_HEAD)]
        yh = oh * lax.rsqrt(jnp.mean(oh * oh, axis=-1, keepdims=True) + NORM_EPS)
        parts.append(yh * hng_ref[...])
    og = (jnp.concatenate(parts, axis=-1) * _silu(r)).astype(BF16)
    out_ref[...] = x_ref[...] + gate * _dot(og, wo_ref[...])


def _gla_layer(x, mod, norm_g, w_in, w_alpha, b_alpha, head_norm_g, w_out):
    batch, seq, _ = x.shape
    ts = GLA_ROW_TILE
    o_q, o_k, o_v, o_g, o_r = 0, GLA_DK, 2 * GLA_DK, 2 * GLA_DK + GLA_DV, 2 * GLA_DK + GLA_DV + GLA_RANK
    w_in_b = w_in.astype(BF16)
    wq, wk, wv = w_in_b[:, o_q:o_k], w_in_b[:, o_k:o_v], w_in_b[:, o_v:o_g]
    wr = w_in_b[:, o_r:]
    wg = jnp.pad(w_in_b[:, o_g:o_r], ((0, 0), (0, LANES - GLA_RANK)))
    wa = jnp.pad(w_alpha, ((0, LANES - GLA_RANK), (0, 0))).astype(BF16)

    def const(shape):
        return pl.BlockSpec(shape, lambda b, t: (0,) * len(shape))

    return pl.pallas_call(
        _gla_kernel,
        grid=(batch, seq // ts),
        in_specs=[
            pl.BlockSpec((None, ts, D_MODEL), lambda b, t: (b, t, 0)),
            pl.BlockSpec((None, 3, D_MODEL), lambda b, t: (b, 0, 0)),
            const((1, D_MODEL)),
            const((D_MODEL, GLA_DK)), const((D_MODEL, GLA_DK)), const((D_MODEL, GLA_DV)),
            const((D_MODEL, GLA_DV)), const((D_MODEL, LANES)),
            const((LANES, GLA_DK)), const((1, GLA_DK)),
            const((1, GLA_DV_HEAD)), const((GLA_DV, D_MODEL)),
        ],
        out_specs=pl.BlockSpec((None, ts, D_MODEL), lambda b, t: (b, t, 0)),
        out_shape=jax.ShapeDtypeStruct(x.shape, F32),
        scratch_shapes=[
            pltpu.VMEM((GLA_HEADS, GLA_DK_HEAD, GLA_DV_HEAD), F32),
            pltpu.VMEM((ts, GLA_DK), F32), pltpu.VMEM((ts, GLA_DK), F32),
            pltpu.VMEM((ts, GLA_DV), F32), pltpu.VMEM((ts, GLA_DK), F32),
            pltpu.VMEM((ts, GLA_DV), F32),
            pltpu.VMEM((ts, D_MODEL), BF16),
        ],
        compiler_params=pltpu.CompilerParams(
            dimension_semantics=("arbitrary", "arbitrary"), vmem_limit_bytes=VMEM_LIMIT_BYTES),
        name="gla_layer",
    )(x, mod, norm_g.reshape(1, D_MODEL), wq, wk, wv, wr, wg, wa, b_alpha.reshape(1, GLA_DK),
      head_norm_g.reshape(1, GLA_DV_HEAD), w_out.astype(BF16))


def _section_group(j):
    return jnp.where(j == 3 * N_GROUPS, 0, j // 3)


def _dsw_in_proj_kernel(x_ref, mod_ref, ng_ref, w_ref, out_ref, h_scr, hf_scr, tmp_scr):
    seq = x_ref.shape[0]
    rc = PROJ_ROW_CHUNK
    n_lane_blocks = D_MODEL // LANES
    j = pl.program_id(1)

    @pl.when(j == 0)
    def _():
        shift, scale = mod_ref[0:1, :], mod_ref[1:2, :]
        g = ng_ref[...]
        for s in range(seq // rc):
            rows = pl.ds(s * rc, rc)
            h = _modulated_rmsnorm(x_ref[rows, :], g, shift, scale)
            h_scr[0, rows, :] = h.astype(BF16)
            for cb in range(n_lane_blocks):
                hf_scr[cb, rows, :] = h[:, cb * LANES:(cb + 1) * LANES]
        for cb in range(n_lane_blocks):
            lanes = pl.ds(cb * LANES, LANES)
            src = hf_scr.at[cb]
            for gi in range(1, N_GROUPS):
                prev_len = seq // DSW_GROUPS[gi - 1][1]
                seg_len = prev_len // REGATHER_STRIDE
                n = min(rc, seg_len)
                for p in range(DSW_GROUPS[gi - 1][1]):
                    for r2 in range(REGATHER_STRIDE):
                        for s in range(seg_len // n):
                            piece = src[pl.ds(p * prev_len + r2 + s * n * REGATHER_STRIDE, n,
                                              stride=REGATHER_STRIDE), :]
                            dst = pl.ds((p * REGATHER_STRIDE + r2) * seg_len + s * n, n)
                            h_scr[gi, dst, lanes] = piece.astype(BF16)
                            if gi + 1 < N_GROUPS:
                                tmp_scr[dst, :] = piece
                src = tmp_scr

    gi = _section_group(j)
    w = w_ref[...]
    is_query = (j % 3 == 0) & (j < 3 * N_GROUPS)
    out_scale = jnp.where(is_query, QUERY_SCALE, 1.0).astype(F32)
    for s in range(seq // rc):
        rows = pl.ds(s * rc, rc)
        out_ref[rows, :] = (_dot(h_scr[gi, rows, :], w) * out_scale).astype(BF16)


def _dsw_in_proj(x, mod, norm_g, w_in):
    batch, seq, _ = x.shape
    n_sec = 3 * N_GROUPS + 1
    return pl.pallas_call(
        _dsw_in_proj_kernel,
        grid=(batch, n_sec),
        in_specs=[
            pl.BlockSpec((None, seq, D_MODEL), lambda b, j: (b, 0, 0)),
            pl.BlockSpec((None, 3, D_MODEL), lambda b, j: (b, 0, 0)),
            pl.BlockSpec((1, D_MODEL), lambda b, j: (0, 0)),
            pl.BlockSpec((D_MODEL, DSW_WIDTH), lambda b, j: (0, j)),
        ],
        out_specs=pl.BlockSpec((None, None, seq, DSW_WIDTH), lambda b, j: (b, j, 0, 0)),
        out_shape=jax.ShapeDtypeStruct((batch, n_sec, seq, DSW_WIDTH), BF16),
        scratch_shapes=[pltpu.VMEM((N_GROUPS, seq, D_MODEL), BF16),
                        pltpu.VMEM((D_MODEL // LANES, seq, LANES), F32),
                        pltpu.VMEM((seq, LANES), F32)],
        compiler_params=pltpu.CompilerParams(
            dimension_semantics=("arbitrary", "arbitrary"), vmem_limit_bytes=VMEM_LIMIT_BYTES),
        name="dsw_in_proj",
    )(x, mod, norm_g.reshape(1, D_MODEL), w_in.astype(BF16))


def _t5_causal_bucket(n):
    max_exact = REL_BUCKETS // 2
    nf = np.maximum(n, 1).astype(np.float32)
    large = max_exact + (np.log(nf / max_exact) / math.log(REL_MAX_DIST / max_exact)
                         * (REL_BUCKETS - max_exact)).astype(np.int32)
    large = np.minimum(large, REL_BUCKETS - 1)
    return np.where(n < max_exact, n, large).astype(np.int32)


def _bucket_tables():
    qi = np.arange(DSW_SPAN)[:, None]
    kj = np.arange(2 * DSW_SPAN)[None, :]
    steps = qi + DSW_SPAN - kj
    in_window = (steps >= 0) & (steps <= DSW_SPAN)
    tables = []
    for _, dil in DSW_GROUPS:
        bucket = _t5_causal_bucket(np.clip(steps, 0, DSW_SPAN) * dil)
        tables.append(np.where(in_window, bucket, -1))
    return np.stack(tables).astype(np.int32)


def _dsw_bias_kernel(rel_ref, bucket_ref, out_ref):
    bucket = bucket_ref[...]
    for h in range(DSW_HEADS):
        acc = jnp.where(bucket < 0, -jnp.inf, 0.0).astype(F32)
        for bk in range(REL_BUCKETS):
            acc = jnp.where(bucket == bk, rel_ref[bk, h], acc)
        out_ref[pl.ds(h * DSW_SPAN, DSW_SPAN), :] = acc * LOG2E


def _dsw_bias(rel_bias):
    return pl.pallas_call(
        _dsw_bias_kernel,
        grid=(N_GROUPS,),
        in_specs=[
            pl.BlockSpec(memory_space=pltpu.SMEM),
            pl.BlockSpec((None, DSW_SPAN, 2 * DSW_SPAN), lambda g: (g, 0, 0)),
        ],
        out_specs=pl.BlockSpec((None, DSW_HEADS * DSW_SPAN, 2 * DSW_SPAN), lambda g: (g, 0, 0)),
        out_shape=jax.ShapeDtypeStruct((N_GROUPS, DSW_HEADS * DSW_SPAN, 2 * DSW_SPAN), F32),
        compiler_params=pltpu.CompilerParams(dimension_semantics=("arbitrary",)),
        name="dsw_bias",
    )(rel_bias, jnp.asarray(_bucket_tables()))


def _dsw_attention_kernel(*refs):
    qkv_refs = refs[:3 * N_GROUPS]
    gate_ref, bias_ref, out_ref, pv_scr, m_scr, l_scr = refs[3 * N_GROUPS:]
    seq = out_ref.shape[0]
    span = DSW_SPAN
    lane = lax.broadcasted_iota(jnp.int32, (span, LANES), 1)
    head_masks = [(lane // DSW_HEAD_DIM) == hh for hh in range(HEADS_PER_STEP)]

    units = [(gi, blk) for gi in range(N_GROUPS) for blk in range(seq // span)]

    def is_first(gi, blk):
        return blk % (seq // DSW_GROUPS[gi][1] // span) == 0

    def key_rows(gi, blk):
        return pl.ds(blk * span, span) if is_first(gi, blk) else pl.ds((blk - 1) * span, 2 * span)

    def scores(gi, blk):
        q = qkv_refs[3 * gi][pl.ds(blk * span, span), :]
        q2 = jnp.concatenate([jnp.where(mk, q, jnp.zeros_like(q)) for mk in head_masks], axis=0)
        return _dot_nt(q2, qkv_refs[3 * gi + 1][key_rows(gi, blk), :])

    s_next = scores(*units[0])
    for idx, (gi, blk) in enumerate(units):
        dil = DSW_GROUPS[gi][1]
        s = s_next
        if idx + 1 < len(units):
            s_next = scores(*units[idx + 1])
        if is_first(gi, blk):
            s = s + bias_ref[gi, :, pl.ds(span, span)]
        else:
            s = s + bias_ref[gi]
        m = jnp.max(s, axis=-1, keepdims=True)
        e = jnp.exp2(s - m)
        l = jnp.sum(e, axis=-1, keepdims=True)
        pv = _dot(e.astype(BF16), qkv_refs[3 * gi + 2][key_rows(gi, blk), :])
        blocks_per_seg = seq // dil // span
        residue = SEG_RESIDUE[gi][blk // blocks_per_seg]
        i0 = (blk % blocks_per_seg) * span
        dst = pl.ds(i0 * dil + residue, span, stride=dil) if dil > 1 else pl.ds(blk * span, span)
        pv_scr[gi, dst, :] = jnp.where(head_masks[0], pv[:span], pv[span:])
        m_scr[gi, dst, :] = jnp.where(head_masks[0], m[:span], m[span:])
        l_scr[gi, dst, :] = jnp.where(head_masks[0], l[:span], l[span:])

    tops = [m_scr[gi] for gi in range(N_GROUPS)]
    top = functools.reduce(jnp.maximum, tops)
    wts = [jnp.exp2(v - top) for v in tops]
    num = functools.reduce(lambda a, b: a + b, [wts[gi] * pv_scr[gi] for gi in range(N_GROUPS)])
    den = functools.reduce(lambda a, b: a + b, [wts[gi] * l_scr[gi] for gi in range(N_GROUPS)])
    out_ref[...] = ((num / den) * _silu(gate_ref[...].astype(F32))).astype(BF16)


def _dsw_attention(proj, bias):
    batch, n_sec, seq, _ = proj.shape
    n_steps = DSW_HEADS // HEADS_PER_STEP

    def sec(j):
        return pl.BlockSpec((None, None, seq, LANES), lambda b, hp, j=j: (b, j, 0, hp))

    return pl.pallas_call(
        _dsw_attention_kernel,
        grid=(batch, n_steps),
        in_specs=[sec(j) for j in range(n_sec)] + [
            pl.BlockSpec((N_GROUPS, HEADS_PER_STEP * DSW_SPAN, 2 * DSW_SPAN), lambda b, hp: (0, hp, 0)),
        ],
        out_specs=pl.BlockSpec((None, seq, LANES), lambda b, hp: (b, 0, hp)),
        out_shape=jax.ShapeDtypeStruct((batch, seq, DSW_WIDTH), BF16),
        scratch_shapes=[pltpu.VMEM((N_GROUPS, seq, LANES), F32)] * 3,
        compiler_params=pltpu.CompilerParams(
            dimension_semantics=("arbitrary", "arbitrary"), vmem_limit_bytes=VMEM_LIMIT_BYTES),
        name="dsw_attention",
    )(*([proj] * n_sec), bias)


def _dsw_out_proj_kernel(o_ref, x_ref, mod_ref, w_ref, fg_ref, out_ref):
    gate = mod_ref[2:3, :]
    xn = x_ref[...] + gate * _dot(o_ref[...], w_ref[...])
    out_ref[...] = xn * lax.rsqrt(jnp.mean(xn * xn, axis=-1, keepdims=True) + NORM_EPS) * fg_ref[...]


def _dsw_out_proj(og, x, mod, w_out, final_g):
    batch, seq, _ = x.shape
    ts = OUT_ROW_TILE
    return pl.pallas_call(
        _dsw_out_proj_kernel,
        grid=(batch, seq // ts),
        in_specs=[
            pl.BlockSpec((None, ts, DSW_WIDTH), lambda b, t: (b, t, 0)),
            pl.BlockSpec((None, ts, D_MODEL), lambda b, t: (b, t, 0)),
            pl.BlockSpec((None, 3, D_MODEL), lambda b, t: (b, 0, 0)),
            pl.BlockSpec((DSW_WIDTH, D_MODEL), lambda b, t: (0, 0)),
            pl.BlockSpec((1, D_MODEL), lambda b, t: (0, 0)),
        ],
        out_specs=pl.BlockSpec((None, ts, D_MODEL), lambda b, t: (b, t, 0)),
        out_shape=jax.ShapeDtypeStruct(x.shape, F32),
        compiler_params=pltpu.CompilerParams(
            dimension_semantics=("arbitrary", "arbitrary"), vmem_limit_bytes=VMEM_LIMIT_BYTES),
        name="dsw_out_proj",
    )(og, x, mod, w_out.astype(BF16), final_g.reshape(1, D_MODEL))


def kernel(x, c, ada_w, ada_b, norm_g, gla_w_in, gla_w_alpha, gla_b_alpha, gla_norm_g, gla_w_out,
           dsw_w_in, dsw_w_out, rel_bias, final_g):
    batch = x.shape[0]
    mod = _adaln_mod(c, ada_w, ada_b)
    mod = mod.reshape(DEPTH, batch, 3, D_MODEL)
    x = _gla_layer(x, mod[0], norm_g[0], gla_w_in[0], gla_w_alpha[0], gla_b_alpha[0], gla_norm_g[0],
                   gla_w_out[0])
    proj = _dsw_in_proj(x, mod[1], norm_g[1], dsw_w_in[0])
    bias = _dsw_bias(rel_bias)
    og = _dsw_attention(proj, bias)
    return _dsw_out_proj(og, x, mod[1], dsw_w_out[0], final_g)
```

```python
import functools
import math

import jax
import jax.numpy as jnp
import numpy as np
from jax import lax
from jax.experimental import pallas as pl
from jax.experimental.pallas import tpu as pltpu

F32 = jnp.float32
BF16 = jnp.bfloat16

D_MODEL = 1024
DEPTH = 2
NORM_EPS = 1e-6

GLA_HEADS = 4
GLA_DK = D_MODEL // 2
GLA_DV = D_MODEL
GLA_DK_HEAD = GLA_DK // GLA_HEADS
GLA_DV_HEAD = GLA_DV // GLA_HEADS
GLA_RANK = 16
GLA_TAU = 16.0
GLA_CHUNK = 64

DSW_HEADS = 16
DSW_HEAD_DIM = D_MODEL // DSW_HEADS
DSW_WIDTH = DSW_HEADS * DSW_HEAD_DIM
DSW_GROUPS = ((128, 1), (512, 4), (2048, 16))
N_GROUPS = len(DSW_GROUPS)
DSW_SPAN = 128
REGATHER_STRIDE = 4
assert N_GROUPS == 3 and all(b[1] == a[1] * REGATHER_STRIDE for a, b in zip(DSW_GROUPS, DSW_GROUPS[1:]))


def _segment_residues():
    table = [[0]]
    for gi in range(1, N_GROUPS):
        prev_dil = DSW_GROUPS[gi - 1][1]
        table.append([res + prev_dil * r2 for res in table[-1] for r2 in range(REGATHER_STRIDE)])
    return table


SEG_RESIDUE = _segment_residues()
REL_BUCKETS = 32
REL_MAX_DIST = 2048

LANES = 128
HEADS_PER_STEP = LANES // DSW_HEAD_DIM
assert HEADS_PER_STEP == 2
LOG2E = math.log2(math.e)
QUERY_SCALE = DSW_HEAD_DIM ** -0.5 * LOG2E
VMEM_LIMIT_BYTES = 56 * 1024 * 1024

GLA_ROW_TILE = 512
OUT_ROW_TILE = 512
PROJ_ROW_CHUNK = 512
PROJ_ROW_TILE = 4096

_NT = (((1,), (1,)), ((), ()))


def _dot(a, b, precision=None):
    return jnp.dot(a, b, preferred_element_type=F32, precision=precision)


def _dot_nt(a, b):
    return lax.dot_general(a, b, _NT, preferred_element_type=F32)


def _silu(v):
    return v * (1.0 / (1.0 + jnp.exp(-v)))


def _log_sigmoid(z):
    return jnp.minimum(z, 0.0) - jnp.log(1.0 + jnp.exp(-jnp.abs(z)))


def _chunk_cumsum(x, chunk):
    pos = lax.broadcasted_iota(jnp.int32, x.shape, 0) % chunk
    step = 1
    while step < chunk:
        x = x + jnp.where(pos >= step, pltpu.roll(x, shift=step, axis=0), 0.0)
        step *= 2
    return x


def _modulated_rmsnorm(x, g, shift, scale):
    y = x * lax.rsqrt(jnp.mean(x * x, axis=-1, keepdims=True) + NORM_EPS)
    return y * g * (1.0 + scale) + shift


def _adaln_kernel(c_ref, w_ref, b_ref, out_ref):
    c_act = _silu(c_ref[...])
    out_ref[...] = _dot(c_act, w_ref[...], precision=lax.Precision.HIGHEST) + b_ref[...]


def _adaln_mod(c, ada_w, ada_b):
    batch = c.shape[0]
    n_col = 3
    return pl.pallas_call(
        _adaln_kernel,
        grid=(DEPTH, n_col),
        in_specs=[
            pl.BlockSpec((batch, D_MODEL), lambda i, j: (0, 0)),
            pl.BlockSpec((None, D_MODEL, D_MODEL), lambda i, j: (i, 0, j)),
            pl.BlockSpec((None, 1, D_MODEL), lambda i, j: (i, 0, j)),
        ],
        out_specs=pl.BlockSpec((None, batch, D_MODEL), lambda i, j: (i, 0, j)),
        out_shape=jax.ShapeDtypeStruct((DEPTH, batch, 3 * D_MODEL), F32),
        compiler_params=pltpu.CompilerParams(
            dimension_semantics=("arbitrary", "arbitrary"), vmem_limit_bytes=VMEM_LIMIT_BYTES),
        name="adaln_mod",
    )(c, ada_w, ada_b.reshape(DEPTH, 1, 3 * D_MODEL))


def _gla_kernel(x_ref, mod_ref, ng_ref, wq_ref, wk_ref, wv_ref, wr_ref, wg_ref, wa_ref, ba_ref,
                hng_ref, wo_ref, next_mod_ref, next_ng_ref, out_ref, *rest):
    h_refs = rest[:N_GROUPS]
    state_ref, q_scr, k_scr, v_scr, b_scr, o_scr, hb_scr, hf_scr, tmp_scr = rest[N_GROUPS:]
    ts = GLA_ROW_TILE
    ch = GLA_CHUNK

    @pl.when(pl.program_id(1) == 0)
    def _():
        state_ref[...] = jnp.zeros_like(state_ref)

    shift, scale, gate = mod_ref[0:1, :], mod_ref[1:2, :], mod_ref[2:3, :]
    hb_scr[...] = _modulated_rmsnorm(x_ref[...], ng_ref[...], shift, scale).astype(BF16)

    g_lr = _dot(hb_scr[...], wg_ref[...])
    z = _dot(g_lr.astype(BF16), wa_ref[...]) + ba_ref[...]
    b_scr[...] = _chunk_cumsum(_log_sigmoid(z) / GLA_TAU, ch)
    q_scr[...] = _dot(hb_scr[...], wq_ref[...]) * (GLA_DK_HEAD ** -0.5)
    k_scr[...] = _dot(hb_scr[...], wk_ref[...])
    v_scr[...] = _dot(hb_scr[...], wv_ref[...])

    ci = lax.broadcasted_iota(jnp.int32, (ch, ch), 0)
    cj = lax.broadcasted_iota(jnp.int32, (ch, ch), 1)
    causal = cj <= ci
    n_chunks = ts // ch
    heads = range(GLA_HEADS)

    def k_lanes(hd):
        return pl.ds(hd * GLA_DK_HEAD, GLA_DK_HEAD)

    def v_lanes(hd):
        return pl.ds(hd * GLA_DV_HEAD, GLA_DV_HEAD)

    def intra_scores(c):
        rows = pl.ds(c * ch, ch)
        out = []
        for hd in heads:
            bc = b_scr[rows, k_lanes(hd)]
            mid = bc[ch // 2:ch // 2 + 1, :]
            qe = (q_scr[rows, k_lanes(hd)] * jnp.exp(bc - mid)).astype(BF16)
            ke = (k_scr[rows, k_lanes(hd)] * jnp.exp(mid - bc)).astype(BF16)
            out.append(_dot_nt(qe, ke))
        return out

    a_next = intra_scores(0)
    for c in range(n_chunks):
        rows = pl.ds(c * ch, ch)
        a_cur = a_next
        if c + 1 < n_chunks:
            a_next = intra_scores(c + 1)
        for hd in heads:
            bc = b_scr[rows, k_lanes(hd)]
            last = bc[ch - 1:ch, :]
            qc = q_scr[rows, k_lanes(hd)]
            kc = k_scr[rows, k_lanes(hd)]
            vcb = v_scr[rows, v_lanes(hd)].astype(BF16)
            st = state_ref[hd]
            kdt = (kc * jnp.exp(last - bc)).T.astype(BF16)
            a = jnp.where(causal, a_cur[hd], 0.0).astype(BF16)
            res = _dot(jnp.concatenate([a, kdt], axis=0), vcb)
            inter = _dot((qc * jnp.exp(bc)).astype(BF16), st.astype(BF16))
            decay = jnp.broadcast_to(jnp.exp(last), (GLA_DK_HEAD, GLA_DK_HEAD)).T
            decay = jnp.concatenate([decay] * (GLA_DV_HEAD // GLA_DK_HEAD), axis=1)
            state_ref[hd] = st * decay + res[ch:]
            o_scr[rows, v_lanes(hd)] = res[:ch] + inter

    r = _dot(hb_scr[...], wr_ref[...])
    parts = []
    for hd in range(GLA_HEADS):
        oh = o_scr[:, pl.ds(hd * GLA_DV_HEAD, GLA_DV_HEAD)]
        yh = oh * lax.rsqrt(jnp.mean(oh * oh, axis=-1, keepdims=True) + NORM_EPS)
        parts.append(yh * hng_ref[...])
    og = (jnp.concatenate(parts, axis=-1) * _silu(r)).astype(BF16)
    x_new = x_ref[...] + gate * _dot(og, wo_ref[...])
    out_ref[...] = x_new
    h_next = _modulated_rmsnorm(x_new, next_ng_ref[...], next_mod_ref[0:1, :], next_mod_ref[1:2, :])
    _store_group_orders(h_next, h_refs, hf_scr, tmp_scr)


def _gla_layer(x, mod, norm_g, w_in, w_alpha, b_alpha, head_norm_g, w_out, next_mod, next_norm_g):
    batch, seq, _ = x.shape
    ts = GLA_ROW_TILE
    o_q, o_k, o_v, o_g, o_r = 0, GLA_DK, 2 * GLA_DK, 2 * GLA_DK + GLA_DV, 2 * GLA_DK + GLA_DV + GLA_RANK
    w_in_b = w_in.astype(BF16)
    wq, wk, wv = w_in_b[:, o_q:o_k], w_in_b[:, o_k:o_v], w_in_b[:, o_v:o_g]
    wr = w_in_b[:, o_r:]
    wg = jnp.pad(w_in_b[:, o_g:o_r], ((0, 0), (0, LANES - GLA_RANK)))
    wa = jnp.pad(w_alpha, ((0, LANES - GLA_RANK), (0, 0))).astype(BF16)

    def const(shape):
        return pl.BlockSpec(shape, lambda b, t: (0,) * len(shape))

    return pl.pallas_call(
        _gla_kernel,
        grid=(batch, seq // ts),
        in_specs=[
            pl.BlockSpec((None, ts, D_MODEL), lambda b, t: (b, t, 0)),
            pl.BlockSpec((None, 3, D_MODEL), lambda b, t: (b, 0, 0)),
            const((1, D_MODEL)),
            const((D_MODEL, GLA_DK)), const((D_MODEL, GLA_DK)), const((D_MODEL, GLA_DV)),
            const((D_MODEL, GLA_DV)), const((D_MODEL, LANES)),
            const((LANES, GLA_DK)), const((1, GLA_DK)),
            const((1, GLA_DV_HEAD)), const((GLA_DV, D_MODEL)),
            pl.BlockSpec((None, 3, D_MODEL), lambda b, t: (b, 0, 0)),
            const((1, D_MODEL)),
        ],
        out_specs=[pl.BlockSpec((None, ts, D_MODEL), lambda b, t: (b, t, 0))] + [
            pl.BlockSpec((None, dil, ts // dil, D_MODEL), lambda b, t: (b, 0, t, 0)) for _, dil in DSW_GROUPS],
        out_shape=[jax.ShapeDtypeStruct(x.shape, F32)] + [
            jax.ShapeDtypeStruct((batch, dil, seq // dil, D_MODEL), BF16) for _, dil in DSW_GROUPS],
        scratch_shapes=[
            pltpu.VMEM((GLA_HEADS, GLA_DK_HEAD, GLA_DV_HEAD), F32),
            pltpu.VMEM((ts, GLA_DK), F32), pltpu.VMEM((ts, GLA_DK), F32),
            pltpu.VMEM((ts, GLA_DV), F32), pltpu.VMEM((ts, GLA_DK), F32),
            pltpu.VMEM((ts, GLA_DV), F32),
            pltpu.VMEM((ts, D_MODEL), BF16),
            pltpu.VMEM((D_MODEL // LANES, ts, LANES), F32),
            pltpu.VMEM((ts, LANES), F32),
        ],
        compiler_params=pltpu.CompilerParams(
            dimension_semantics=("arbitrary", "arbitrary"), vmem_limit_bytes=VMEM_LIMIT_BYTES),
        name="gla_layer",
    )(x, mod, norm_g.reshape(1, D_MODEL), wq, wk, wv, wr, wg, wa, b_alpha.reshape(1, GLA_DK),
      head_norm_g.reshape(1, GLA_DV_HEAD), w_out.astype(BF16), next_mod, next_norm_g.reshape(1, D_MODEL))


def _store_group_orders(h, out_refs, hf_scr, tmp_scr):
    n_rows = h.shape[0]
    out_refs[0][0] = h.astype(BF16)
    for cb in range(D_MODEL // LANES):
        hf_scr[cb] = h[:, cb * LANES:(cb + 1) * LANES]
    for cb in range(D_MODEL // LANES):
        lanes = pl.ds(cb * LANES, LANES)
        src = hf_scr.at[cb]
        for gi in range(1, N_GROUPS):
            prev_dil = DSW_GROUPS[gi - 1][1]
            prev_len = n_rows // prev_dil
            seg_len = prev_len // REGATHER_STRIDE
            for p in range(prev_dil):
                for r2 in range(REGATHER_STRIDE):
                    seg = p * REGATHER_STRIDE + r2
                    piece = src[pl.ds(p * prev_len + r2, seg_len, stride=REGATHER_STRIDE), :]
                    out_refs[gi][seg, :, lanes] = piece.astype(BF16)
                    if gi + 1 < N_GROUPS:
                        tmp_scr[pl.ds(seg * seg_len, seg_len), :] = piece
            src = tmp_scr


def _dsw_proj_kernel(h_ref, w_ref, out_ref):
    out_scale = jnp.where(pl.program_id(1) == 0, QUERY_SCALE, 1.0).astype(F32)
    w = w_ref[...]
    for s in range(h_ref.shape[0] // PROJ_ROW_CHUNK):
        rows = pl.ds(s * PROJ_ROW_CHUNK, PROJ_ROW_CHUNK)
        out_ref[rows, :] = (_dot(h_ref[rows, :], w) * out_scale).astype(BF16)


def _dsw_group_proj(h, w_sections):
    m_rows = h.shape[0]
    n_sec = w_sections.shape[1] // DSW_WIDTH
    tm = PROJ_ROW_TILE
    return pl.pallas_call(
        _dsw_proj_kernel,
        grid=(m_rows // tm, n_sec),
        in_specs=[
            pl.BlockSpec((tm, D_MODEL), lambda i, j: (i, 0)),
            pl.BlockSpec((D_MODEL, DSW_WIDTH), lambda i, j: (0, j)),
        ],
        out_specs=pl.BlockSpec((None, tm, DSW_WIDTH), lambda i, j: (j, i, 0)),
        out_shape=jax.ShapeDtypeStruct((n_sec, m_rows, DSW_WIDTH), BF16),
        compiler_params=pltpu.CompilerParams(
            dimension_semantics=("arbitrary", "arbitrary"), vmem_limit_bytes=VMEM_LIMIT_BYTES),
        name="dsw_group_proj",
    )(h, w_sections)


def _t5_causal_bucket(n):
    max_exact = REL_BUCKETS // 2
    nf = np.maximum(n, 1).astype(np.float32)
    large = max_exact + (np.log(nf / max_exact) / math.log(REL_MAX_DIST / max_exact)
                         * (REL_BUCKETS - max_exact)).astype(np.int32)
    large = np.minimum(large, REL_BUCKETS - 1)
    return np.where(n < max_exact, n, large).astype(np.int32)


def _bucket_tables():
    qi = np.arange(DSW_SPAN)[:, None]
    kj = np.arange(2 * DSW_SPAN)[None, :]
    steps = qi + DSW_SPAN - kj
    in_window = (steps >= 0) & (steps <= DSW_SPAN)
    tables = []
    for _, dil in DSW_GROUPS:
        bucket = _t5_causal_bucket(np.clip(steps, 0, DSW_SPAN) * dil)
        tables.append(np.where(in_window, bucket, -1))
    return np.stack(tables).astype(np.int32)


def _dsw_bias_kernel(rel_ref, bucket_ref, out_ref):
    bucket = bucket_ref[...]
    for h in range(DSW_HEADS):
        acc = jnp.where(bucket < 0, -jnp.inf, 0.0).astype(F32)
        for bk in range(REL_BUCKETS):
            acc = jnp.where(bucket == bk, rel_ref[bk, h], acc)
        out_ref[pl.ds(h * DSW_SPAN, DSW_SPAN), :] = acc * LOG2E


def _dsw_bias(rel_bias):
    return pl.pallas_call(
        _dsw_bias_kernel,
        grid=(N_GROUPS,),
        in_specs=[
            pl.BlockSpec(memory_space=pltpu.SMEM),
            pl.BlockSpec((None, DSW_SPAN, 2 * DSW_SPAN), lambda g: (g, 0, 0)),
        ],
        out_specs=pl.BlockSpec((None, DSW_HEADS * DSW_SPAN, 2 * DSW_SPAN), lambda g: (g, 0, 0)),
        out_shape=jax.ShapeDtypeStruct((N_GROUPS, DSW_HEADS * DSW_SPAN, 2 * DSW_SPAN), F32),
        compiler_params=pltpu.CompilerParams(dimension_semantics=("arbitrary",)),
        name="dsw_bias",
    )(rel_bias, jnp.asarray(_bucket_tables()))


def _dsw_attention_kernel(*refs):
    qkv_refs = refs[:3 * N_GROUPS]
    gate_ref, bias_ref, out_ref, pv_scr, m_scr, l_scr = refs[3 * N_GROUPS:]
    seq = out_ref.shape[0]
    span = DSW_SPAN
    lane = lax.broadcasted_iota(jnp.int32, (span, LANES), 1)
    head_masks = [(lane // DSW_HEAD_DIM) == hh for hh in range(HEADS_PER_STEP)]

    units = [(gi, blk) for gi in range(N_GROUPS) for blk in range(seq // span)]

    def is_first(gi, blk):
        return blk % (seq // DSW_GROUPS[gi][1] // span) == 0

    def key_rows(gi, blk):
        return pl.ds(blk * span, span) if is_first(gi, blk) else pl.ds((blk - 1) * span, 2 * span)

    def scores(gi, blk):
        q = qkv_refs[3 * gi][pl.ds(blk * span, span), :]
        q2 = jnp.concatenate([jnp.where(mk, q, jnp.zeros_like(q)) for mk in head_masks], axis=0)
        return _dot_nt(q2, qkv_refs[3 * gi + 1][key_rows(gi, blk), :])

    s_next = scores(*units[0])
    for idx, (gi, blk) in enumerate(units):
        dil = DSW_GROUPS[gi][1]
        s = s_next
        if idx + 1 < len(units):
            s_next = scores(*units[idx + 1])
        if is_first(gi, blk):
            s = s + bias_ref[gi, :, pl.ds(span, span)]
        else:
            s = s + bias_ref[gi]
        m = jnp.max(s, axis=-1, keepdims=True)
        e = jnp.exp2(s - m)
        l = jnp.sum(e, axis=-1, keepdims=True)
        pv = _dot(e.astype(BF16), qkv_refs[3 * gi + 2][key_rows(gi, blk), :])
        blocks_per_seg = seq // dil // span
        residue = SEG_RESIDUE[gi][blk // blocks_per_seg]
        i0 = (blk % blocks_per_seg) * span
        dst = pl.ds(i0 * dil + residue, span, stride=dil) if dil > 1 else pl.ds(blk * span, span)
        pv_scr[gi, dst, :] = jnp.where(head_masks[0], pv[:span], pv[span:])
        m_scr[gi, dst, :] = jnp.where(head_masks[0], m[:span], m[span:])
        l_scr[gi, dst, :] = jnp.where(head_masks[0], l[:span], l[span:])

    tops = [m_scr[gi] for gi in range(N_GROUPS)]
    top = functools.reduce(jnp.maximum, tops)
    wts = [jnp.exp2(v - top) for v in tops]
    num = functools.reduce(lambda a, b: a + b, [wts[gi] * pv_scr[gi] for gi in range(N_GROUPS)])
    den = functools.reduce(lambda a, b: a + b, [wts[gi] * l_scr[gi] for gi in range(N_GROUPS)])
    out_ref[...] = ((num / den) * _silu(gate_ref[...].astype(F32))).astype(BF16)


def _dsw_attention(projs, bias, batch):
    seq = projs[0].shape[1] // batch
    n_steps = DSW_HEADS // HEADS_PER_STEP
    operands, in_specs = [], []

    def add(arr, j):
        operands.append(arr.reshape(arr.shape[0], batch, seq, DSW_WIDTH))
        in_specs.append(pl.BlockSpec((None, None, seq, LANES), lambda b, hp, j=j: (j, b, 0, hp)))

    for gi in range(N_GROUPS):
        for j in range(3):
            add(projs[gi], j)
    add(projs[0], 3)
    in_specs.append(
        pl.BlockSpec((N_GROUPS, HEADS_PER_STEP * DSW_SPAN, 2 * DSW_SPAN), lambda b, hp: (0, hp, 0)))
    return pl.pallas_call(
        _dsw_attention_kernel,
        grid=(batch, n_steps),
        in_specs=in_specs,
        out_specs=pl.BlockSpec((None, seq, LANES), lambda b, hp: (b, 0, hp)),
        out_shape=jax.ShapeDtypeStruct((batch, seq, DSW_WIDTH), BF16),
        scratch_shapes=[pltpu.VMEM((N_GROUPS, seq, LANES), F32)] * 3,
        compiler_params=pltpu.CompilerParams(
            dimension_semantics=("arbitrary", "arbitrary"), vmem_limit_bytes=VMEM_LIMIT_BYTES),
        name="dsw_attention",
    )(*operands, bias)


def _dsw_out_proj_kernel(o_ref, x_ref, mod_ref, w_ref, fg_ref, out_ref):
    gate = mod_ref[2:3, :]
    xn = x_ref[...] + gate * _dot(o_ref[...], w_ref[...])
    out_ref[...] = xn * lax.rsqrt(jnp.mean(xn * xn, axis=-1, keepdims=True) + NORM_EPS) * fg_ref[...]


def _dsw_out_proj(og, x, mod, w_out, final_g):
    batch, seq, _ = x.shape
    ts = OUT_ROW_TILE
    return pl.pallas_call(
        _dsw_out_proj_kernel,
        grid=(batch, seq // ts),
        in_specs=[
            pl.BlockSpec((None, ts, DSW_WIDTH), lambda b, t: (b, t, 0)),
            pl.BlockSpec((None, ts, D_MODEL), lambda b, t: (b, t, 0)),
            pl.BlockSpec((None, 3, D_MODEL), lambda b, t: (b, 0, 0)),
            pl.BlockSpec((DSW_WIDTH, D_MODEL), lambda b, t: (0, 0)),
            pl.BlockSpec((1, D_MODEL), lambda b, t: (0, 0)),
        ],
        out_specs=pl.BlockSpec((None, ts, D_MODEL), lambda b, t: (b, t, 0)),
        out_shape=jax.ShapeDtypeStruct(x.shape, F32),
        compiler_params=pltpu.CompilerParams(
            dimension_semantics=("arbitrary", "arbitrary"), vmem_limit_bytes=VMEM_LIMIT_BYTES),
        name="dsw_out_proj",
    )(og, x, mod, w_out.astype(BF16), final_g.reshape(1, D_MODEL))


def kernel(x, c, ada_w, ada_b, norm_g, gla_w_in, gla_w_alpha, gla_b_alpha, gla_norm_g, gla_w_out,
           dsw_w_in, dsw_w_out, rel_bias, final_g):
    batch = x.shape[0]
    mod = _adaln_mod(c, ada_w, ada_b)
    mod = mod.reshape(DEPTH, batch, 3, D_MODEL)
    x, *h_orders = _gla_layer(x, mod[0], norm_g[0], gla_w_in[0], gla_w_alpha[0], gla_b_alpha[0],
                              gla_norm_g[0], gla_w_out[0], mod[1], norm_g[1])
    w_in = dsw_w_in[0].astype(BF16)
    n_qkv = 3 * DSW_WIDTH
    projs = []
    for gi in range(N_GROUPS):
        w_g = w_in[:, gi * n_qkv:(gi + 1) * n_qkv]
        if gi == 0:
            w_g = jnp.concatenate([w_g, w_in[:, N_GROUPS * n_qkv:]], axis=1)
        projs.append(_dsw_group_proj(h_orders[gi].reshape(-1, D_MODEL), w_g))
    bias = _dsw_bias(rel_bias)
    og = _dsw_attention(projs, bias, batch)
    return _dsw_out_proj(og, x, mod[1], dsw_w_out[0], final_g)
```

```python
import functools
import math

import jax
import jax.numpy as jnp
import numpy as np
from jax import lax
from jax.experimental import pallas as pl
from jax.experimental.pallas import tpu as pltpu

F32 = jnp.float32
BF16 = jnp.bfloat16

D_MODEL = 1024
DEPTH = 2
NORM_EPS = 1e-6

GLA_HEADS = 4
GLA_DK = D_MODEL // 2
GLA_DV = D_MODEL
GLA_DK_HEAD = GLA_DK // GLA_HEADS
GLA_DV_HEAD = GLA_DV // GLA_HEADS
GLA_RANK = 16
GLA_TAU = 16.0
GLA_CHUNK = 64

DSW_HEADS = 16
DSW_HEAD_DIM = D_MODEL // DSW_HEADS
DSW_WIDTH = DSW_HEADS * DSW_HEAD_DIM
DSW_GROUPS = ((128, 1), (512, 4), (2048, 16))
N_GROUPS = len(DSW_GROUPS)
DSW_SPAN = 128
REGATHER_STRIDE = 4
assert N_GROUPS == 3 and all(b[1] == a[1] * REGATHER_STRIDE for a, b in zip(DSW_GROUPS, DSW_GROUPS[1:]))


def _segment_residues():
    table = [[0]]
    for gi in range(1, N_GROUPS):
        prev_dil = DSW_GROUPS[gi - 1][1]
        table.append([res + prev_dil * r2 for res in table[-1] for r2 in range(REGATHER_STRIDE)])
    return table


SEG_RESIDUE = _segment_residues()
REL_BUCKETS = 32
REL_MAX_DIST = 2048

LANES = 128
HEADS_PER_STEP = LANES // DSW_HEAD_DIM
assert HEADS_PER_STEP == 2
PAIRS_PER_STEP = 1
LOG2E = math.log2(math.e)
QUERY_SCALE = DSW_HEAD_DIM ** -0.5 * LOG2E
VMEM_LIMIT_BYTES = 56 * 1024 * 1024

GLA_ROW_TILE = 512
GLA_SUB_TILE = 512
OUT_ROW_TILE = 2048
PROJ_ROW_CHUNK = 512
PROJ_ROW_TILE = 4096

_NT = (((1,), (1,)), ((), ()))


def _dot(a, b, precision=None):
    return jnp.dot(a, b, preferred_element_type=F32, precision=precision)


def _dot_nt(a, b):
    return lax.dot_general(a, b, _NT, preferred_element_type=F32)


def _silu(v):
    return v * (1.0 / (1.0 + jnp.exp(-v)))


def _log_sigmoid(z):
    return jnp.minimum(z, 0.0) - jnp.log(1.0 + jnp.exp(-jnp.abs(z)))


def _chunk_cumsum(x, chunk):
    pos = lax.broadcasted_iota(jnp.int32, x.shape, 0) % chunk
    step = 1
    while step < chunk:
        x = x + jnp.where(pos >= step, pltpu.roll(x, shift=step, axis=0), 0.0)
        step *= 2
    return x


def _modulated_rmsnorm(x, g, shift, scale):
    y = x * lax.rsqrt(jnp.mean(x * x, axis=-1, keepdims=True) + NORM_EPS)
    return y * g * (1.0 + scale) + shift


def _adaln_kernel(c_ref, w_ref, b_ref, out_ref):
    c_act = _silu(c_ref[...])
    out_ref[...] = _dot(c_act, w_ref[...], precision=lax.Precision.HIGHEST) + b_ref[...]


def _adaln_mod(c, ada_w, ada_b):
    batch = c.shape[0]
    n_col = 3
    return pl.pallas_call(
        _adaln_kernel,
        grid=(DEPTH, n_col),
        in_specs=[
            pl.BlockSpec((batch, D_MODEL), lambda i, j: (0, 0)),
            pl.BlockSpec((None, D_MODEL, D_MODEL), lambda i, j: (i, 0, j)),
            pl.BlockSpec((None, 1, D_MODEL), lambda i, j: (i, 0, j)),
        ],
        out_specs=pl.BlockSpec((None, batch, D_MODEL), lambda i, j: (i, 0, j)),
        out_shape=jax.ShapeDtypeStruct((DEPTH, batch, 3 * D_MODEL), F32),
        compiler_params=pltpu.CompilerParams(
            dimension_semantics=("arbitrary", "arbitrary"), vmem_limit_bytes=VMEM_LIMIT_BYTES),
        name="adaln_mod",
    )(c, ada_w, ada_b.reshape(DEPTH, 1, 3 * D_MODEL))


def _gla_kernel(x_ref, mod_ref, ng_ref, wq_ref, wk_ref, wv_ref, wr_ref, wg_ref, wa_ref, ba_ref,
                hng_ref, wo_ref, next_mod_ref, next_ng_ref, out_ref, *rest):
    h_refs = rest[:N_GROUPS]
    state_ref, q_scr, k_scr, v_scr, b_scr, o_scr, hb_scr, hf_scr, tmp_scr = rest[N_GROUPS:]
    ts = GLA_ROW_TILE
    ch = GLA_CHUNK

    @pl.when(pl.program_id(1) == 0)
    def _():
        state_ref[...] = jnp.zeros_like(state_ref)

    shift, scale, gate = mod_ref[0:1, :], mod_ref[1:2, :], mod_ref[2:3, :]
    sub = GLA_SUB_TILE
    n_sub = ts // sub

    def project(i):
        rows = pl.ds(i * sub, sub)
        hb_scr[rows, :] = _modulated_rmsnorm(x_ref[rows, :], ng_ref[...], shift, scale).astype(BF16)
        hb = hb_scr[rows, :]
        g_lr = _dot(hb, wg_ref[...])
        z = _dot(g_lr.astype(BF16), wa_ref[...]) + ba_ref[...]
        b_scr[rows, :] = _chunk_cumsum(_log_sigmoid(z) / GLA_TAU, ch)
        q_scr[rows, :] = _dot(hb, wq_ref[...]) * (GLA_DK_HEAD ** -0.5)
        k_scr[rows, :] = _dot(hb, wk_ref[...])
        v_scr[rows, :] = _dot(hb, wv_ref[...])

    for i in range(n_sub):
        project(i)

    ci = lax.broadcasted_iota(jnp.int32, (ch, ch), 0)
    cj = lax.broadcasted_iota(jnp.int32, (ch, ch), 1)
    causal = cj <= ci
    n_chunks = ts // ch
    heads = range(GLA_HEADS)

    def k_lanes(hd):
        return pl.ds(hd * GLA_DK_HEAD, GLA_DK_HEAD)

    def v_lanes(hd):
        return pl.ds(hd * GLA_DV_HEAD, GLA_DV_HEAD)

    def intra_scores(c):
        rows = pl.ds(c * ch, ch)
        out = []
        for hd in heads:
            bc = b_scr[rows, k_lanes(hd)]
            mid = bc[ch // 2:ch // 2 + 1, :]
            qe = (q_scr[rows, k_lanes(hd)] * jnp.exp(bc - mid)).astype(BF16)
            ke = (k_scr[rows, k_lanes(hd)] * jnp.exp(mid - bc)).astype(BF16)
            out.append(_dot_nt(qe, ke))
        return out

    a_next = intra_scores(0)
    for c in range(n_chunks):
        rows = pl.ds(c * ch, ch)
        a_cur = a_next
        if c + 1 < n_chunks:
            a_next = intra_scores(c + 1)
        for hd in heads:
            bc = b_scr[rows, k_lanes(hd)]
            last = bc[ch - 1:ch, :]
            qc = q_scr[rows, k_lanes(hd)]
            kc = k_scr[rows, k_lanes(hd)]
            vcb = v_scr[rows, v_lanes(hd)].astype(BF16)
            st = state_ref[hd]
            kdt = (kc * jnp.exp(last - bc)).T.astype(BF16)
            a = jnp.where(causal, a_cur[hd], 0.0).astype(BF16)
            res = _dot(jnp.concatenate([a, kdt], axis=0), vcb)
            inter = _dot((qc * jnp.exp(bc)).astype(BF16), st.astype(BF16))
            decay = jnp.broadcast_to(jnp.exp(last), (GLA_DK_HEAD, GLA_DK_HEAD)).T
            decay = jnp.concatenate([decay] * (GLA_DV_HEAD // GLA_DK_HEAD), axis=1)
            state_ref[hd] = st * decay + res[ch:]
            o_scr[rows, v_lanes(hd)] = res[:ch] + inter

    for i in range(n_sub):
        rows = pl.ds(i * sub, sub)
        r = _dot(hb_scr[rows, :], wr_ref[...])
        parts = []
        for hd in range(GLA_HEADS):
            oh = o_scr[rows, pl.ds(hd * GLA_DV_HEAD, GLA_DV_HEAD)]
            yh = oh * lax.rsqrt(jnp.mean(oh * oh, axis=-1, keepdims=True) + NORM_EPS)
            parts.append(yh * hng_ref[...])
        og = (jnp.concatenate(parts, axis=-1) * _silu(r)).astype(BF16)
        x_new = x_ref[rows, :] + gate * _dot(og, wo_ref[...])
        out_ref[rows, :] = x_new
        h_next = _modulated_rmsnorm(x_new, next_ng_ref[...], next_mod_ref[0:1, :], next_mod_ref[1:2, :])
        _store_group_orders(h_next, i * sub, h_refs, hf_scr, tmp_scr)


def _gla_layer(x, mod, norm_g, w_in, w_alpha, b_alpha, head_norm_g, w_out, next_mod, next_norm_g):
    batch, seq, _ = x.shape
    ts = GLA_ROW_TILE
    o_q, o_k, o_v, o_g, o_r = 0, GLA_DK, 2 * GLA_DK, 2 * GLA_DK + GLA_DV, 2 * GLA_DK + GLA_DV + GLA_RANK
    w_in_b = w_in.astype(BF16)
    wq, wk, wv = w_in_b[:, o_q:o_k], w_in_b[:, o_k:o_v], w_in_b[:, o_v:o_g]
    wr = w_in_b[:, o_r:]
    wg = jnp.pad(w_in_b[:, o_g:o_r], ((0, 0), (0, LANES - GLA_RANK)))
    wa = jnp.pad(w_alpha, ((0, LANES - GLA_RANK), (0, 0))).astype(BF16)

    def const(shape):
        return pl.BlockSpec(shape, lambda b, t: (0,) * len(shape))

    return pl.pallas_call(
        _gla_kernel,
        grid=(batch, seq // ts),
        in_specs=[
            pl.BlockSpec((None, ts, D_MODEL), lambda b, t: (b, t, 0)),
            pl.BlockSpec((None, 3, D_MODEL), lambda b, t: (b, 0, 0)),
            const((1, D_MODEL)),
            const((D_MODEL, GLA_DK)), const((D_MODEL, GLA_DK)), const((D_MODEL, GLA_DV)),
            const((D_MODEL, GLA_DV)), const((D_MODEL, LANES)),
            const((LANES, GLA_DK)), const((1, GLA_DK)),
            const((1, GLA_DV_HEAD)), const((GLA_DV, D_MODEL)),
            pl.BlockSpec((None, 3, D_MODEL), lambda b, t: (b, 0, 0)),
            const((1, D_MODEL)),
        ],
        out_specs=[pl.BlockSpec((None, ts, D_MODEL), lambda b, t: (b, t, 0))] + [
            pl.BlockSpec((None, dil, ts // dil, D_MODEL), lambda b, t: (b, 0, t, 0)) for _, dil in DSW_GROUPS],
        out_shape=[jax.ShapeDtypeStruct(x.shape, F32)] + [
            jax.ShapeDtypeStruct((batch, dil, seq // dil, D_MODEL), BF16) for _, dil in DSW_GROUPS],
        scratch_shapes=[
            pltpu.VMEM((GLA_HEADS, GLA_DK_HEAD, GLA_DV_HEAD), F32),
            pltpu.VMEM((ts, GLA_DK), F32), pltpu.VMEM((ts, GLA_DK), F32),
            pltpu.VMEM((ts, GLA_DV), F32), pltpu.VMEM((ts, GLA_DK), F32),
            pltpu.VMEM((ts, GLA_DV), F32),
            pltpu.VMEM((ts, D_MODEL), BF16),
            pltpu.VMEM((D_MODEL // LANES, GLA_SUB_TILE, LANES), F32),
            pltpu.VMEM((GLA_SUB_TILE, LANES), F32),
        ],
        compiler_params=pltpu.CompilerParams(
            dimension_semantics=("arbitrary", "arbitrary"), vmem_limit_bytes=VMEM_LIMIT_BYTES),
        name="gla_layer",
    )(x, mod, norm_g.reshape(1, D_MODEL), wq, wk, wv, wr, wg, wa, b_alpha.reshape(1, GLA_DK),
      head_norm_g.reshape(1, GLA_DV_HEAD), w_out.astype(BF16), next_mod, next_norm_g.reshape(1, D_MODEL))


def _store_group_orders(h, row0, out_refs, hf_scr, tmp_scr):
    n_rows = h.shape[0]
    out_refs[0][0, pl.ds(row0, n_rows), :] = h.astype(BF16)
    for cb in range(D_MODEL // LANES):
        hf_scr[cb] = h[:, cb * LANES:(cb + 1) * LANES]
    for cb in range(D_MODEL // LANES):
        lanes = pl.ds(cb * LANES, LANES)
        src = hf_scr.at[cb]
        for gi in range(1, N_GROUPS):
            prev_dil = DSW_GROUPS[gi - 1][1]
            prev_len = n_rows // prev_dil
            seg_len = prev_len // REGATHER_STRIDE
            for p in range(prev_dil):
                for r2 in range(REGATHER_STRIDE):
                    seg = p * REGATHER_STRIDE + r2
                    piece = src[pl.ds(p * prev_len + r2, seg_len, stride=REGATHER_STRIDE), :]
                    out_refs[gi][seg, pl.ds(row0 // DSW_GROUPS[gi][1], seg_len), lanes] = piece.astype(BF16)
                    if gi + 1 < N_GROUPS:
                        tmp_scr[pl.ds(seg * seg_len, seg_len), :] = piece
            src = tmp_scr


def _dsw_proj_kernel(h_ref, w_ref, out_ref):
    out_scale = jnp.where(pl.program_id(1) == 0, QUERY_SCALE, 1.0).astype(F32)
    w = w_ref[...].astype(BF16)
    for s in range(h_ref.shape[0] // PROJ_ROW_CHUNK):
        rows = pl.ds(s * PROJ_ROW_CHUNK, PROJ_ROW_CHUNK)
        out_ref[rows, :] = (_dot(h_ref[rows, :], w) * out_scale).astype(BF16)


def _dsw_group_proj(h, w_in, sections):
    m_rows = h.shape[0]
    n_sec = len(sections)
    tm = PROJ_ROW_TILE

    def w_block(i, j):
        col = functools.reduce(lambda a, b: a + b, [jnp.where(j == k, sec, 0) for k, sec in enumerate(sections)])
        return (0, 0, col)

    return pl.pallas_call(
        _dsw_proj_kernel,
        grid=(m_rows // tm, n_sec),
        in_specs=[
            pl.BlockSpec((tm, D_MODEL), lambda i, j: (i, 0)),
            pl.BlockSpec((None, D_MODEL, DSW_WIDTH), w_block),
        ],
        out_specs=pl.BlockSpec((None, tm, DSW_WIDTH), lambda i, j: (j, i, 0)),
        out_shape=jax.ShapeDtypeStruct((n_sec, m_rows, DSW_WIDTH), BF16),
        compiler_params=pltpu.CompilerParams(
            dimension_semantics=("arbitrary", "arbitrary"), vmem_limit_bytes=VMEM_LIMIT_BYTES),
        name="dsw_group_proj",
    )(h, w_in)


def _t5_causal_bucket(n):
    max_exact = REL_BUCKETS // 2
    nf = np.maximum(n, 1).astype(np.float32)
    large = max_exact + (np.log(nf / max_exact) / math.log(REL_MAX_DIST / max_exact)
                         * (REL_BUCKETS - max_exact)).astype(np.int32)
    large = np.minimum(large, REL_BUCKETS - 1)
    return np.where(n < max_exact, n, large).astype(np.int32)


def _bucket_tables():
    qi = np.arange(DSW_SPAN)[:, None]
    kj = np.arange(2 * DSW_SPAN)[None, :]
    steps = qi + DSW_SPAN - kj
    in_window = (steps >= 0) & (steps <= DSW_SPAN)
    tables = []
    for _, dil in DSW_GROUPS:
        bucket = _t5_causal_bucket(np.clip(steps, 0, DSW_SPAN) * dil)
        tables.append(np.where(in_window, bucket, -1))
    return np.stack(tables).astype(np.int32)


def _dsw_bias_kernel(rel_ref, bucket_ref, out_ref):
    bucket = bucket_ref[...]
    masks = [bucket == bk for bk in range(REL_BUCKETS)]
    outside = jnp.where(bucket < 0, -jnp.inf, 0.0).astype(F32)
    for h in range(DSW_HEADS):
        acc = outside
        for bk in range(REL_BUCKETS):
            acc = jnp.where(masks[bk], rel_ref[bk, h] * LOG2E, acc)
        out_ref[pl.ds(h * DSW_SPAN, DSW_SPAN), :] = acc


def _dsw_bias(rel_bias):
    return pl.pallas_call(
        _dsw_bias_kernel,
        grid=(N_GROUPS,),
        in_specs=[
            pl.BlockSpec(memory_space=pltpu.SMEM),
            pl.BlockSpec((None, DSW_SPAN, 2 * DSW_SPAN), lambda g: (g, 0, 0)),
        ],
        out_specs=pl.BlockSpec((None, DSW_HEADS * DSW_SPAN, 2 * DSW_SPAN), lambda g: (g, 0, 0)),
        out_shape=jax.ShapeDtypeStruct((N_GROUPS, DSW_HEADS * DSW_SPAN, 2 * DSW_SPAN), F32),
        compiler_params=pltpu.CompilerParams(dimension_semantics=("arbitrary",)),
        name="dsw_bias",
    )(rel_bias, jnp.asarray(_bucket_tables()))


def _dsw_attention_kernel(*refs):
    qkv_refs = refs[:3 * N_GROUPS]
    gate_ref, bias_ref, out_ref, pv_scr, m_scr, l_scr = refs[3 * N_GROUPS:]
    seq = out_ref.shape[0]
    span = DSW_SPAN
    lane = lax.broadcasted_iota(jnp.int32, (span, LANES), 1)
    head_masks = [(lane // DSW_HEAD_DIM) == hh for hh in range(HEADS_PER_STEP)]

    def is_first(gi, blk):
        return blk % (seq // DSW_GROUPS[gi][1] // span) == 0

    def key_rows(gi, blk):
        return pl.ds(blk * span, span) if is_first(gi, blk) else pl.ds((blk - 1) * span, 2 * span)

    def scores(pair, gi, blk):
        lanes = pl.ds(pair * LANES, LANES)
        q = qkv_refs[3 * gi][pl.ds(blk * span, span), lanes]
        q2 = jnp.concatenate([jnp.where(mk, q, jnp.zeros_like(q)) for mk in head_masks], axis=0)
        return _dot_nt(q2, qkv_refs[3 * gi + 1][key_rows(gi, blk), lanes])

    units = [(pair, gi, blk) for pair in range(PAIRS_PER_STEP) for gi in range(N_GROUPS)
             for blk in range(seq // span)]
    s_next = scores(*units[0])
    for idx, (pair, gi, blk) in enumerate(units):
        dil = DSW_GROUPS[gi][1]
        lanes = pl.ds(pair * LANES, LANES)
        bias_rows = pl.ds(pair * HEADS_PER_STEP * span, HEADS_PER_STEP * span)
        s = s_next
        if idx + 1 < len(units):
            s_next = scores(*units[idx + 1])
        if is_first(gi, blk):
            s = s + bias_ref[gi, bias_rows, pl.ds(span, span)]
        else:
            s = s + bias_ref[gi, bias_rows, :]
        m = jnp.max(s, axis=-1, keepdims=True)
        e = jnp.exp2(s - m)
        l = jnp.sum(e, axis=-1, keepdims=True)
        pv = _dot(e.astype(BF16), qkv_refs[3 * gi + 2][key_rows(gi, blk), lanes])
        blocks_per_seg = seq // dil // span
        residue = SEG_RESIDUE[gi][blk // blocks_per_seg]
        i0 = (blk % blocks_per_seg) * span
        dst = pl.ds(i0 * dil + residue, span, stride=dil) if dil > 1 else pl.ds(blk * span, span)
        pv_scr[gi, dst, :] = jnp.where(head_masks[0], pv[:span], pv[span:])
        m_scr[gi, dst, :] = jnp.where(head_masks[0], m[:span], m[span:])
        l_scr[gi, dst, :] = jnp.where(head_masks[0], l[:span], l[span:])

        if (gi, blk) == (N_GROUPS - 1, seq // span - 1):
            tops = [m_scr[g] for g in range(N_GROUPS)]
            top = functools.reduce(jnp.maximum, tops)
            wts = [jnp.exp2(v - top) for v in tops]
            num = functools.reduce(lambda a, b: a + b, [wts[g] * pv_scr[g] for g in range(N_GROUPS)])
            den = functools.reduce(lambda a, b: a + b, [wts[g] * l_scr[g] for g in range(N_GROUPS)])
            out_ref[:, lanes] = ((num / den) * _silu(gate_ref[:, lanes].astype(F32))).astype(BF16)


def _dsw_attention(projs, bias, batch):
    seq = projs[0].shape[1] // batch
    heads_per_step = HEADS_PER_STEP * PAIRS_PER_STEP
    step_lanes = LANES * PAIRS_PER_STEP
    n_steps = DSW_HEADS // heads_per_step
    operands, in_specs = [], []

    def add(arr, j):
        operands.append(arr.reshape(arr.shape[0], batch, seq, DSW_WIDTH))
        in_specs.append(pl.BlockSpec((None, None, seq, step_lanes), lambda b, hp, j=j: (j, b, 0, hp)))

    for gi in range(N_GROUPS):
        for j in range(3):
            add(projs[gi], j)
    add(projs[0], 3)
    in_specs.append(
        pl.BlockSpec((N_GROUPS, heads_per_step * DSW_SPAN, 2 * DSW_SPAN), lambda b, hp: (0, hp, 0)))
    return pl.pallas_call(
        _dsw_attention_kernel,
        grid=(batch, n_steps),
        in_specs=in_specs,
        out_specs=pl.BlockSpec((None, seq, step_lanes), lambda b, hp: (b, 0, hp)),
        out_shape=jax.ShapeDtypeStruct((batch, seq, DSW_WIDTH), BF16),
        scratch_shapes=[pltpu.VMEM((N_GROUPS, seq, LANES), F32)] * 3,
        compiler_params=pltpu.CompilerParams(
            dimension_semantics=("arbitrary", "arbitrary"), vmem_limit_bytes=VMEM_LIMIT_BYTES),
        name="dsw_attention",
    )(*operands, bias)


def _dsw_out_proj_kernel(o_ref, x_ref, mod_ref, w_ref, fg_ref, out_ref):
    gate = mod_ref[2:3, :]
    w = w_ref[...].astype(BF16)
    for s in range(o_ref.shape[0] // PROJ_ROW_CHUNK):
        rows = pl.ds(s * PROJ_ROW_CHUNK, PROJ_ROW_CHUNK)
        xn = x_ref[rows, :] + gate * _dot(o_ref[rows, :], w)
        out_ref[rows, :] = xn * lax.rsqrt(jnp.mean(xn * xn, axis=-1, keepdims=True) + NORM_EPS) * fg_ref[...]


def _dsw_out_proj(og, x, mod, w_out, final_g):
    batch, seq, _ = x.shape
    ts = OUT_ROW_TILE
    return pl.pallas_call(
        _dsw_out_proj_kernel,
        grid=(batch, seq // ts),
        in_specs=[
            pl.BlockSpec((None, ts, DSW_WIDTH), lambda b, t: (b, t, 0)),
            pl.BlockSpec((None, ts, D_MODEL), lambda b, t: (b, t, 0)),
            pl.BlockSpec((None, 3, D_MODEL), lambda b, t: (b, 0, 0)),
            pl.BlockSpec((DSW_WIDTH, D_MODEL), lambda b, t: (0, 0)),
            pl.BlockSpec((1, D_MODEL), lambda b, t: (0, 0)),
        ],
        out_specs=pl.BlockSpec((None, ts, D_MODEL), lambda b, t: (b, t, 0)),
        out_shape=jax.ShapeDtypeStruct(x.shape, F32),
        compiler_params=pltpu.CompilerParams(
            dimension_semantics=("arbitrary", "arbitrary"), vmem_limit_bytes=VMEM_LIMIT_BYTES),
        name="dsw_out_proj",
    )(og, x, mod, w_out, final_g.reshape(1, D_MODEL))


def kernel(x, c, ada_w, ada_b, norm_g, gla_w_in, gla_w_alpha, gla_b_alpha, gla_norm_g, gla_w_out,
           dsw_w_in, dsw_w_out, rel_bias, final_g):
    batch = x.shape[0]
    mod = _adaln_mod(c, ada_w, ada_b)
    mod = mod.reshape(DEPTH, batch, 3, D_MODEL)
    x, *h_orders = _gla_layer(x, mod[0], norm_g[0], gla_w_in[0], gla_w_alpha[0], gla_b_alpha[0],
                              gla_norm_g[0], gla_w_out[0], mod[1], norm_g[1])
    projs = []
    for gi in range(N_GROUPS):
        sections = [3 * gi, 3 * gi + 1, 3 * gi + 2]
        if gi == 0:
            sections.append(3 * N_GROUPS)
        projs.append(_dsw_group_proj(h_orders[gi].reshape(-1, D_MODEL), dsw_w_in, sections))
    bias = _dsw_bias(rel_bias)
    og = _dsw_attention(projs, bias, batch)
    return _dsw_out_proj(og, x, mod[1], dsw_w_out[0], final_g)
```

```python
import functools
import math

import jax
import jax.numpy as jnp
import numpy as np
from jax import lax
from jax.experimental import pallas as pl
from jax.experimental.pallas import tpu as pltpu

F32 = jnp.float32
BF16 = jnp.bfloat16

D_MODEL = 1024
DEPTH = 2
NORM_EPS = 1e-6

GLA_HEADS = 4
GLA_DK = D_MODEL // 2
GLA_DV = D_MODEL
GLA_DK_HEAD = GLA_DK // GLA_HEADS
GLA_DV_HEAD = GLA_DV // GLA_HEADS
GLA_RANK = 16
GLA_TAU = 16.0
GLA_CHUNK = 64

DSW_HEADS = 16
DSW_HEAD_DIM = D_MODEL // DSW_HEADS
DSW_WIDTH = DSW_HEADS * DSW_HEAD_DIM
DSW_GROUPS = ((128, 1), (512, 4), (2048, 16))
N_GROUPS = len(DSW_GROUPS)
DSW_SPAN = 128
REGATHER_STRIDE = 4
assert N_GROUPS == 3 and all(b[1] == a[1] * REGATHER_STRIDE for a, b in zip(DSW_GROUPS, DSW_GROUPS[1:]))


def _segment_residues():
    table = [[0]]
    for gi in range(1, N_GROUPS):
        prev_dil = DSW_GROUPS[gi - 1][1]
        table.append([res + prev_dil * r2 for res in table[-1] for r2 in range(REGATHER_STRIDE)])
    return table


SEG_RESIDUE = _segment_residues()
REL_BUCKETS = 32
REL_MAX_DIST = 2048

LANES = 128
HEADS_PER_STEP = LANES // DSW_HEAD_DIM
assert HEADS_PER_STEP == 2
ATT_BATCH_PER_STEP = 2
LOG2E = math.log2(math.e)
QUERY_SCALE = DSW_HEAD_DIM ** -0.5 * LOG2E
VMEM_LIMIT_BYTES = 56 * 1024 * 1024

GLA_ROW_TILE = 512
GLA_SUB_TILE = 512
OUT_ROW_TILE = 2048
PROJ_ROW_CHUNK = 512
PROJ_ROW_TILE = 4096

_NT = (((1,), (1,)), ((), ()))


def _dot(a, b, precision=None):
    return jnp.dot(a, b, preferred_element_type=F32, precision=precision)


def _dot_nt(a, b):
    return lax.dot_general(a, b, _NT, preferred_element_type=F32)


def _silu(v):
    return v * (1.0 / (1.0 + jnp.exp(-v)))


def _log_sigmoid(z):
    return jnp.minimum(z, 0.0) - jnp.log(1.0 + jnp.exp(-jnp.abs(z)))


def _chunk_cumsum(x, chunk):
    pos = lax.broadcasted_iota(jnp.int32, x.shape, 0) % chunk
    step = 1
    while step < chunk:
        x = x + jnp.where(pos >= step, pltpu.roll(x, shift=step, axis=0), 0.0)
        step *= 2
    return x


def _modulated_rmsnorm(x, g, shift, scale):
    y = x * lax.rsqrt(jnp.mean(x * x, axis=-1, keepdims=True) + NORM_EPS)
    return y * g * (1.0 + scale) + shift


def _adaln_kernel(c_ref, w_ref, b_ref, out_ref):
    c_act = _silu(c_ref[...])
    out_ref[...] = _dot(c_act, w_ref[...], precision=lax.Precision.HIGHEST) + b_ref[...]


def _adaln_mod(c, ada_w, ada_b):
    batch = c.shape[0]
    n_col = 3
    return pl.pallas_call(
        _adaln_kernel,
        grid=(DEPTH, n_col),
        in_specs=[
            pl.BlockSpec((batch, D_MODEL), lambda i, j: (0, 0)),
            pl.BlockSpec((None, D_MODEL, D_MODEL), lambda i, j: (i, 0, j)),
            pl.BlockSpec((None, 1, D_MODEL), lambda i, j: (i, 0, j)),
        ],
        out_specs=pl.BlockSpec((None, batch, D_MODEL), lambda i, j: (i, 0, j)),
        out_shape=jax.ShapeDtypeStruct((DEPTH, batch, 3 * D_MODEL), F32),
        compiler_params=pltpu.CompilerParams(
            dimension_semantics=("arbitrary", "arbitrary"), vmem_limit_bytes=VMEM_LIMIT_BYTES),
        name="adaln_mod",
    )(c, ada_w, ada_b.reshape(DEPTH, 1, 3 * D_MODEL))


def _gla_kernel(x_ref, mod_ref, ng_ref, wq_ref, wk_ref, wv_ref, wr_ref, wg_ref, wa_ref, ba_ref,
                hng_ref, wo_ref, next_mod_ref, next_ng_ref, out_ref, *rest):
    h_refs = rest[:N_GROUPS]
    state_ref, q_scr, k_scr, v_scr, b_scr, o_scr, hb_scr, hf_scr, tmp_scr = rest[N_GROUPS:]
    ts = GLA_ROW_TILE
    ch = GLA_CHUNK

    @pl.when(pl.program_id(1) == 0)
    def _():
        state_ref[...] = jnp.zeros_like(state_ref)

    shift, scale, gate = mod_ref[0:1, :], mod_ref[1:2, :], mod_ref[2:3, :]
    sub = GLA_SUB_TILE
    n_sub = ts // sub

    def project(i):
        rows = pl.ds(i * sub, sub)
        hb_scr[rows, :] = _modulated_rmsnorm(x_ref[rows, :], ng_ref[...], shift, scale).astype(BF16)
        hb = hb_scr[rows, :]
        g_lr = _dot(hb, wg_ref[...])
        z = _dot(g_lr.astype(BF16), wa_ref[...]) + ba_ref[...]
        b_scr[rows, :] = _chunk_cumsum(_log_sigmoid(z) / GLA_TAU, ch)
        q_scr[rows, :] = _dot(hb, wq_ref[...]) * (GLA_DK_HEAD ** -0.5)
        k_scr[rows, :] = _dot(hb, wk_ref[...])
        v_scr[rows, :] = _dot(hb, wv_ref[...])

    for i in range(n_sub):
        project(i)

    ci = lax.broadcasted_iota(jnp.int32, (ch, ch), 0)
    cj = lax.broadcasted_iota(jnp.int32, (ch, ch), 1)
    causal = cj <= ci
    n_chunks = ts // ch
    heads = range(GLA_HEADS)

    def k_lanes(hd):
        return pl.ds(hd * GLA_DK_HEAD, GLA_DK_HEAD)

    def v_lanes(hd):
        return pl.ds(hd * GLA_DV_HEAD, GLA_DV_HEAD)

    def intra_scores(c):
        rows = pl.ds(c * ch, ch)
        out = []
        for hd in heads:
            bc = b_scr[rows, k_lanes(hd)]
            mid = bc[ch // 2:ch // 2 + 1, :]
            qe = (q_scr[rows, k_lanes(hd)] * jnp.exp(bc - mid)).astype(BF16)
            ke = (k_scr[rows, k_lanes(hd)] * jnp.exp(mid - bc)).astype(BF16)
            out.append(_dot_nt(qe, ke))
        return out

    a_next = intra_scores(0)
    for c in range(n_chunks):
        rows = pl.ds(c * ch, ch)
        a_cur = a_next
        if c + 1 < n_chunks:
            a_next = intra_scores(c + 1)
        for hd in heads:
            bc = b_scr[rows, k_lanes(hd)]
            last = bc[ch - 1:ch, :]
            qc = q_scr[rows, k_lanes(hd)]
            kc = k_scr[rows, k_lanes(hd)]
            vcb = v_scr[rows, v_lanes(hd)].astype(BF16)
            st = state_ref[hd]
            kdt = (kc * jnp.exp(last - bc)).T.astype(BF16)
            a = jnp.where(causal, a_cur[hd], 0.0).astype(BF16)
            res = _dot(jnp.concatenate([a, kdt], axis=0), vcb)
            inter = _dot((qc * jnp.exp(bc)).astype(BF16), st.astype(BF16))
            decay = jnp.broadcast_to(jnp.exp(last), (GLA_DK_HEAD, GLA_DK_HEAD)).T
            decay = jnp.concatenate([decay] * (GLA_DV_HEAD // GLA_DK_HEAD), axis=1)
            state_ref[hd] = st * decay + res[ch:]
            o_scr[rows, v_lanes(hd)] = res[:ch] + inter

    for i in range(n_sub):
        rows = pl.ds(i * sub, sub)
        r = _dot(hb_scr[rows, :], wr_ref[...])
        parts = []
        for hd in range(GLA_HEADS):
            oh = o_scr[rows, pl.ds(hd * GLA_DV_HEAD, GLA_DV_HEAD)]
            yh = oh * lax.rsqrt(jnp.mean(oh * oh, axis=-1, keepdims=True) + NORM_EPS)
            parts.append(yh * hng_ref[...])
        og = (jnp.concatenate(parts, axis=-1) * _silu(r)).astype(BF16)
        x_new = x_ref[rows, :] + gate * _dot(og, wo_ref[...])
        out_ref[rows, :] = x_new
        h_next = _modulated_rmsnorm(x_new, next_ng_ref[...], next_mod_ref[0:1, :], next_mod_ref[1:2, :])
        _store_group_orders(h_next, i * sub, h_refs, hf_scr, tmp_scr)


def _gla_layer(x, mod, norm_g, w_in, w_alpha, b_alpha, head_norm_g, w_out, next_mod, next_norm_g):
    batch, seq, _ = x.shape
    ts = GLA_ROW_TILE
    o_q, o_k, o_v, o_g, o_r = 0, GLA_DK, 2 * GLA_DK, 2 * GLA_DK + GLA_DV, 2 * GLA_DK + GLA_DV + GLA_RANK
    w_in_b = w_in.astype(BF16)
    wq, wk, wv = w_in_b[:, o_q:o_k], w_in_b[:, o_k:o_v], w_in_b[:, o_v:o_g]
    wr = w_in_b[:, o_r:]
    wg = jnp.pad(w_in_b[:, o_g:o_r], ((0, 0), (0, LANES - GLA_RANK)))
    wa = jnp.pad(w_alpha, ((0, LANES - GLA_RANK), (0, 0))).astype(BF16)

    def const(shape):
        return pl.BlockSpec(shape, lambda b, t: (0,) * len(shape))

    return pl.pallas_call(
        _gla_kernel,
        grid=(batch, seq // ts),
        in_specs=[
            pl.BlockSpec((None, ts, D_MODEL), lambda b, t: (b, t, 0)),
            pl.BlockSpec((None, 3, D_MODEL), lambda b, t: (b, 0, 0)),
            const((1, D_MODEL)),
            const((D_MODEL, GLA_DK)), const((D_MODEL, GLA_DK)), const((D_MODEL, GLA_DV)),
            const((D_MODEL, GLA_DV)), const((D_MODEL, LANES)),
            const((LANES, GLA_DK)), const((1, GLA_DK)),
            const((1, GLA_DV_HEAD)), const((GLA_DV, D_MODEL)),
            pl.BlockSpec((None, 3, D_MODEL), lambda b, t: (b, 0, 0)),
            const((1, D_MODEL)),
        ],
        out_specs=[pl.BlockSpec((None, ts, D_MODEL), lambda b, t: (b, t, 0))] + [
            pl.BlockSpec((None, dil, ts // dil, D_MODEL), lambda b, t: (b, 0, t, 0)) for _, dil in DSW_GROUPS],
        out_shape=[jax.ShapeDtypeStruct(x.shape, F32)] + [
            jax.ShapeDtypeStruct((batch, dil, seq // dil, D_MODEL), BF16) for _, dil in DSW_GROUPS],
        scratch_shapes=[
            pltpu.VMEM((GLA_HEADS, GLA_DK_HEAD, GLA_DV_HEAD), F32),
            pltpu.VMEM((ts, GLA_DK), F32), pltpu.VMEM((ts, GLA_DK), F32),
            pltpu.VMEM((ts, GLA_DV), F32), pltpu.VMEM((ts, GLA_DK), F32),
            pltpu.VMEM((ts, GLA_DV), F32),
            pltpu.VMEM((ts, D_MODEL), BF16),
            pltpu.VMEM((D_MODEL // LANES, GLA_SUB_TILE, LANES), F32),
            pltpu.VMEM((GLA_SUB_TILE, LANES), F32),
        ],
        compiler_params=pltpu.CompilerParams(
            dimension_semantics=("arbitrary", "arbitrary"), vmem_limit_bytes=VMEM_LIMIT_BYTES),
        name="gla_layer",
    )(x, mod, norm_g.reshape(1, D_MODEL), wq, wk, wv, wr, wg, wa, b_alpha.reshape(1, GLA_DK),
      head_norm_g.reshape(1, GLA_DV_HEAD), w_out.astype(BF16), next_mod, next_norm_g.reshape(1, D_MODEL))


def _store_group_orders(h, row0, out_refs, hf_scr, tmp_scr):
    n_rows = h.shape[0]
    out_refs[0][0, pl.ds(row0, n_rows), :] = h.astype(BF16)
    for cb in range(D_MODEL // LANES):
        hf_scr[cb] = h[:, cb * LANES:(cb + 1) * LANES]
    for cb in range(D_MODEL // LANES):
        lanes = pl.ds(cb * LANES, LANES)
        src = hf_scr.at[cb]
        for gi in range(1, N_GROUPS):
            prev_dil = DSW_GROUPS[gi - 1][1]
            prev_len = n_rows // prev_dil
            seg_len = prev_len // REGATHER_STRIDE
            for p in range(prev_dil):
                for r2 in range(REGATHER_STRIDE):
                    seg = p * REGATHER_STRIDE + r2
                    piece = src[pl.ds(p * prev_len + r2, seg_len, stride=REGATHER_STRIDE), :]
                    out_refs[gi][seg, pl.ds(row0 // DSW_GROUPS[gi][1], seg_len), lanes] = piece.astype(BF16)
                    if gi + 1 < N_GROUPS:
                        tmp_scr[pl.ds(seg * seg_len, seg_len), :] = piece
            src = tmp_scr


def _dsw_proj_kernel(h_ref, w_ref, out_ref):
    out_scale = jnp.where(pl.program_id(1) == 0, QUERY_SCALE, 1.0).astype(F32)
    w = w_ref[...].astype(BF16)
    for s in range(h_ref.shape[0] // PROJ_ROW_CHUNK):
        rows = pl.ds(s * PROJ_ROW_CHUNK, PROJ_ROW_CHUNK)
        out_ref[rows, :] = (_dot(h_ref[rows, :], w) * out_scale).astype(BF16)


def _dsw_group_proj(h, w_in, sections):
    m_rows = h.shape[0]
    n_sec = len(sections)
    tm = PROJ_ROW_TILE

    def w_block(i, j):
        col = functools.reduce(lambda a, b: a + b, [jnp.where(j == k, sec, 0) for k, sec in enumerate(sections)])
        return (0, 0, col)

    return pl.pallas_call(
        _dsw_proj_kernel,
        grid=(m_rows // tm, n_sec),
        in_specs=[
            pl.BlockSpec((tm, D_MODEL), lambda i, j: (i, 0)),
            pl.BlockSpec((None, D_MODEL, DSW_WIDTH), w_block),
        ],
        out_specs=pl.BlockSpec((None, tm, DSW_WIDTH), lambda i, j: (j, i, 0)),
        out_shape=jax.ShapeDtypeStruct((n_sec, m_rows, DSW_WIDTH), BF16),
        compiler_params=pltpu.CompilerParams(
            dimension_semantics=("arbitrary", "arbitrary"), vmem_limit_bytes=VMEM_LIMIT_BYTES),
        name="dsw_group_proj",
    )(h, w_in)


def _t5_causal_bucket(n):
    max_exact = REL_BUCKETS // 2
    nf = np.maximum(n, 1).astype(np.float32)
    large = max_exact + (np.log(nf / max_exact) / math.log(REL_MAX_DIST / max_exact)
                         * (REL_BUCKETS - max_exact)).astype(np.int32)
    large = np.minimum(large, REL_BUCKETS - 1)
    return np.where(n < max_exact, n, large).astype(np.int32)


def _bucket_tables():
    qi = np.arange(DSW_SPAN)[:, None]
    kj = np.arange(2 * DSW_SPAN)[None, :]
    steps = qi + DSW_SPAN - kj
    in_window = (steps >= 0) & (steps <= DSW_SPAN)
    tables = []
    for _, dil in DSW_GROUPS:
        bucket = _t5_causal_bucket(np.clip(steps, 0, DSW_SPAN) * dil)
        tables.append(np.where(in_window, bucket, -1))
    return np.stack(tables).astype(np.int32)


def _dsw_bias_kernel(rel_ref, bucket_ref, out_ref):
    bucket = bucket_ref[...]
    masks = [bucket == bk for bk in range(REL_BUCKETS)]
    outside = jnp.where(bucket < 0, -jnp.inf, 0.0).astype(F32)
    for h in range(DSW_HEADS):
        acc = outside
        for bk in range(REL_BUCKETS):
            acc = jnp.where(masks[bk], rel_ref[bk, h] * LOG2E, acc)
        out_ref[pl.ds(h * DSW_SPAN, DSW_SPAN), :] = acc


def _dsw_bias(rel_bias):
    return pl.pallas_call(
        _dsw_bias_kernel,
        grid=(N_GROUPS,),
        in_specs=[
            pl.BlockSpec(memory_space=pltpu.SMEM),
            pl.BlockSpec((None, DSW_SPAN, 2 * DSW_SPAN), lambda g: (g, 0, 0)),
        ],
        out_specs=pl.BlockSpec((None, DSW_HEADS * DSW_SPAN, 2 * DSW_SPAN), lambda g: (g, 0, 0)),
        out_shape=jax.ShapeDtypeStruct((N_GROUPS, DSW_HEADS * DSW_SPAN, 2 * DSW_SPAN), F32),
        compiler_params=pltpu.CompilerParams(dimension_semantics=("arbitrary",)),
        name="dsw_bias",
    )(rel_bias, jnp.asarray(_bucket_tables()))


def _dsw_attention_kernel(*refs):
    qkv_refs = refs[:3 * N_GROUPS]
    gate_ref, bias_ref, out_ref, pv_scr, m_scr, l_scr = refs[3 * N_GROUPS:]
    n_batch, seq, _ = out_ref.shape
    span = DSW_SPAN
    lane = lax.broadcasted_iota(jnp.int32, (span, LANES), 1)
    head_masks = [(lane // DSW_HEAD_DIM) == hh for hh in range(HEADS_PER_STEP)]

    def is_first(gi, blk):
        return blk % (seq // DSW_GROUPS[gi][1] // span) == 0

    def key_rows(gi, blk):
        return pl.ds(blk * span, span) if is_first(gi, blk) else pl.ds((blk - 1) * span, 2 * span)

    def attend(bb):
        def scores(gi, blk):
            q = qkv_refs[3 * gi][bb, pl.ds(blk * span, span), :]
            q2 = jnp.concatenate([jnp.where(mk, q, jnp.zeros_like(q)) for mk in head_masks], axis=0)
            return _dot_nt(q2, qkv_refs[3 * gi + 1][bb, key_rows(gi, blk), :])

        units = [(gi, blk) for gi in range(N_GROUPS) for blk in range(seq // span)]
        s_next = scores(*units[0])
        for idx, (gi, blk) in enumerate(units):
            dil = DSW_GROUPS[gi][1]
            s = s_next
            if idx + 1 < len(units):
                s_next = scores(*units[idx + 1])
            if is_first(gi, blk):
                s = s + bias_ref[gi, :, pl.ds(span, span)]
            else:
                s = s + bias_ref[gi]
            m = jnp.max(s, axis=-1, keepdims=True)
            e = jnp.exp2(s - m)
            l = jnp.sum(e, axis=-1, keepdims=True)
            pv = _dot(e.astype(BF16), qkv_refs[3 * gi + 2][bb, key_rows(gi, blk), :])
            blocks_per_seg = seq // dil // span
            residue = SEG_RESIDUE[gi][blk // blocks_per_seg]
            i0 = (blk % blocks_per_seg) * span
            dst = pl.ds(i0 * dil + residue, span, stride=dil) if dil > 1 else pl.ds(blk * span, span)
            pv_scr[gi, dst, :] = jnp.where(head_masks[0], pv[:span], pv[span:])
            m_scr[gi, dst, :] = jnp.where(head_masks[0], m[:span], m[span:])
            l_scr[gi, dst, :] = jnp.where(head_masks[0], l[:span], l[span:])

        tops = [m_scr[g] for g in range(N_GROUPS)]
        top = functools.reduce(jnp.maximum, tops)
        wts = [jnp.exp2(v - top) for v in tops]
        num = functools.reduce(lambda a, b: a + b, [wts[g] * pv_scr[g] for g in range(N_GROUPS)])
        den = functools.reduce(lambda a, b: a + b, [wts[g] * l_scr[g] for g in range(N_GROUPS)])
        out_ref[bb] = (num * _silu(gate_ref[bb].astype(F32)) / den).astype(BF16)

    def body(bb, carry):
        attend(bb)
        return carry

    lax.fori_loop(0, n_batch, body, 0)


def _dsw_attention(projs, bias, batch):
    seq = projs[0].shape[1] // batch
    n_pairs = DSW_HEADS // HEADS_PER_STEP
    bps = ATT_BATCH_PER_STEP
    operands, in_specs = [], []

    def add(arr, j):
        operands.append(arr.reshape(arr.shape[0], batch, seq, DSW_WIDTH))
        in_specs.append(pl.BlockSpec((None, bps, seq, LANES), lambda b, hp, j=j: (j, b, 0, hp)))

    for gi in range(N_GROUPS):
        for j in range(3):
            add(projs[gi], j)
    add(projs[0], 3)
    in_specs.append(
        pl.BlockSpec((N_GROUPS, HEADS_PER_STEP * DSW_SPAN, 2 * DSW_SPAN), lambda b, hp: (0, hp, 0)))
    return pl.pallas_call(
        _dsw_attention_kernel,
        grid=(batch // bps, n_pairs),
        in_specs=in_specs,
        out_specs=pl.BlockSpec((bps, seq, LANES), lambda b, hp: (b, 0, hp)),
        out_shape=jax.ShapeDtypeStruct((batch, seq, DSW_WIDTH), BF16),
        scratch_shapes=[pltpu.VMEM((N_GROUPS, seq, LANES), F32)] * 3,
        compiler_params=pltpu.CompilerParams(
            dimension_semantics=("arbitrary", "arbitrary"), vmem_limit_bytes=VMEM_LIMIT_BYTES),
        name="dsw_attention",
    )(*operands, bias)


def _dsw_out_proj_kernel(o_ref, x_ref, mod_ref, w_ref, fg_ref, out_ref):
    gate = mod_ref[2:3, :]
    w = w_ref[...].astype(BF16)
    for s in range(o_ref.shape[0] // PROJ_ROW_CHUNK):
        rows = pl.ds(s * PROJ_ROW_CHUNK, PROJ_ROW_CHUNK)
        xn = x_ref[rows, :] + gate * _dot(o_ref[rows, :], w)
        out_ref[rows, :] = xn * lax.rsqrt(jnp.mean(xn * xn, axis=-1, keepdims=True) + NORM_EPS) * fg_ref[...]


def _dsw_out_proj(og, x, mod, w_out, final_g):
    batch, seq, _ = x.shape
    ts = OUT_ROW_TILE
    return pl.pallas_call(
        _dsw_out_proj_kernel,
        grid=(batch, seq // ts),
        in_specs=[
            pl.BlockSpec((None, ts, DSW_WIDTH), lambda b, t: (b, t, 0)),
            pl.BlockSpec((None, ts, D_MODEL), lambda b, t: (b, t, 0)),
            pl.BlockSpec((None, 3, D_MODEL), lambda b, t: (b, 0, 0)),
            pl.BlockSpec((DSW_WIDTH, D_MODEL), lambda b, t: (0, 0)),
            pl.BlockSpec((1, D_MODEL), lambda b, t: (0, 0)),
        ],
        out_specs=pl.BlockSpec((None, ts, D_MODEL), lambda b, t: (b, t, 0)),
        out_shape=jax.ShapeDtypeStruct(x.shape, F32),
        compiler_params=pltpu.CompilerParams(
            dimension_semantics=("arbitrary", "arbitrary"), vmem_limit_bytes=VMEM_LIMIT_BYTES),
        name="dsw_out_proj",
    )(og, x, mod, w_out, final_g.reshape(1, D_MODEL))


def kernel(x, c, ada_w, ada_b, norm_g, gla_w_in, gla_w_alpha, gla_b_alpha, gla_norm_g, gla_w_out,
           dsw_w_in, dsw_w_out, rel_bias, final_g):
    batch = x.shape[0]
    mod = _adaln_mod(c, ada_w, ada_b)
    mod = mod.reshape(DEPTH, batch, 3, D_MODEL)
    x, *h_orders = _gla_layer(x, mod[0], norm_g[0], gla_w_in[0], gla_w_alpha[0], gla_b_alpha[0],
                              gla_norm_g[0], gla_w_out[0], mod[1], norm_g[1])
    projs = []
    for gi in range(N_GROUPS):
        sections = [3 * gi, 3 * gi + 1, 3 * gi + 2]
        if gi == 0:
            sections.append(3 * N_GROUPS)
        projs.append(_dsw_group_proj(h_orders[gi].reshape(-1, D_MODEL), dsw_w_in, sections))
    bias = _dsw_bias(rel_bias)
    og = _dsw_attention(projs, bias, batch)
    return _dsw_out_proj(og, x, mod[1], dsw_w_out[0], final_g)
```

```python
import functools
import math

import jax
import jax.numpy as jnp
import numpy as np
from jax import lax
from jax.experimental import pallas as pl
from jax.experimental.pallas import tpu as pltpu

F32 = jnp.float32
BF16 = jnp.bfloat16

D_MODEL = 1024
DEPTH = 2
NORM_EPS = 1e-6

GLA_HEADS = 4
GLA_DK = D_MODEL // 2
GLA_DV = D_MODEL
GLA_DK_HEAD = GLA_DK // GLA_HEADS
GLA_DV_HEAD = GLA_DV // GLA_HEADS
GLA_RANK = 16
GLA_TAU = 16.0
GLA_CHUNK = 64

DSW_HEADS = 16
DSW_HEAD_DIM = D_MODEL // DSW_HEADS
DSW_WIDTH = DSW_HEADS * DSW_HEAD_DIM
DSW_GROUPS = ((128, 1), (512, 4), (2048, 16))
N_GROUPS = len(DSW_GROUPS)
DSW_SPAN = 128
REGATHER_STRIDE = 4
assert N_GROUPS == 3 and all(b[1] == a[1] * REGATHER_STRIDE for a, b in zip(DSW_GROUPS, DSW_GROUPS[1:]))


def _segment_residues():
    table = [[0]]
    for gi in range(1, N_GROUPS):
        prev_dil = DSW_GROUPS[gi - 1][1]
        table.append([res + prev_dil * r2 for res in table[-1] for r2 in range(REGATHER_STRIDE)])
    return table


SEG_RESIDUE = _segment_residues()
REL_BUCKETS = 32
REL_MAX_DIST = 2048

LANES = 128
HEADS_PER_STEP = LANES // DSW_HEAD_DIM
assert HEADS_PER_STEP == 2
ATT_BATCH_PER_STEP = 2
LOG2E = math.log2(math.e)
QUERY_SCALE = DSW_HEAD_DIM ** -0.5 * LOG2E
VMEM_LIMIT_BYTES = 56 * 1024 * 1024

GLA_ROW_TILE = 512
GLA_SUB_TILE = 512
OUT_ROW_TILE = 2048
PROJ_ROW_CHUNK = 512
PROJ_ROW_TILE = 4096

_NT = (((1,), (1,)), ((), ()))


def _dot(a, b, precision=None):
    return jnp.dot(a, b, preferred_element_type=F32, precision=precision)


def _dot_nt(a, b):
    return lax.dot_general(a, b, _NT, preferred_element_type=F32)


def _silu(v):
    return v * (1.0 / (1.0 + jnp.exp(-v)))


def _log_sigmoid(z):
    return jnp.minimum(z, 0.0) - jnp.log(1.0 + jnp.exp(-jnp.abs(z)))


def _chunk_cumsum(x, chunk):
    pos = lax.broadcasted_iota(jnp.int32, x.shape, 0) % chunk
    step = 1
    while step < chunk:
        x = x + jnp.where(pos >= step, pltpu.roll(x, shift=step, axis=0), 0.0)
        step *= 2
    return x


def _modulated_rmsnorm(x, g, shift, scale):
    y = x * lax.rsqrt(jnp.mean(x * x, axis=-1, keepdims=True) + NORM_EPS)
    return y * g * (1.0 + scale) + shift


def _adaln_kernel(c_ref, w_ref, b_ref, out_ref):
    c_act = _silu(c_ref[...])
    out_ref[...] = _dot(c_act, w_ref[...], precision=lax.Precision.HIGHEST) + b_ref[...]


def _adaln_mod(c, ada_w, ada_b):
    batch = c.shape[0]
    n_col = 3
    return pl.pallas_call(
        _adaln_kernel,
        grid=(DEPTH, n_col),
        in_specs=[
            pl.BlockSpec((batch, D_MODEL), lambda i, j: (0, 0)),
            pl.BlockSpec((None, D_MODEL, D_MODEL), lambda i, j: (i, 0, j)),
            pl.BlockSpec((None, 1, D_MODEL), lambda i, j: (i, 0, j)),
        ],
        out_specs=pl.BlockSpec((None, batch, D_MODEL), lambda i, j: (i, 0, j)),
        out_shape=jax.ShapeDtypeStruct((DEPTH, batch, 3 * D_MODEL), F32),
        compiler_params=pltpu.CompilerParams(
            dimension_semantics=("arbitrary", "arbitrary"), vmem_limit_bytes=VMEM_LIMIT_BYTES),
        name="adaln_mod",
    )(c, ada_w, ada_b.reshape(DEPTH, 1, 3 * D_MODEL))


def _gla_kernel(x_ref, mod_ref, ng_ref, wq_ref, wk_ref, wv_ref, wr_ref, wg_ref, wa_ref, ba_ref,
                hng_ref, wo_ref, next_mod_ref, next_ng_ref, out_ref, *rest):
    h_refs = rest[:N_GROUPS]
    state_ref, q_scr, k_scr, v_scr, b_scr, o_scr, hb_scr, hf_scr, tmp_scr = rest[N_GROUPS:]
    ts = GLA_ROW_TILE
    ch = GLA_CHUNK

    @pl.when(pl.program_id(1) == 0)
    def _():
        state_ref[...] = jnp.zeros_like(state_ref)

    shift, scale, gate = mod_ref[0:1, :], mod_ref[1:2, :], mod_ref[2:3, :]
    sub = GLA_SUB_TILE
    n_sub = ts // sub

    def project(i):
        rows = pl.ds(i * sub, sub)
        hb_scr[rows, :] = _modulated_rmsnorm(x_ref[rows, :], ng_ref[...], shift, scale).astype(BF16)
        hb = hb_scr[rows, :]
        g_lr = _dot(hb, wg_ref[...])
        z = _dot(g_lr.astype(BF16), wa_ref[...]) + ba_ref[...]
        b_scr[rows, :] = _chunk_cumsum(_log_sigmoid(z) / GLA_TAU, ch)
        q_scr[rows, :] = _dot(hb, wq_ref[...]) * (GLA_DK_HEAD ** -0.5)
        k_scr[rows, :] = _dot(hb, wk_ref[...])
        v_scr[rows, :] = _dot(hb, wv_ref[...])

    for i in range(n_sub):
        project(i)

    ci = lax.broadcasted_iota(jnp.int32, (ch, ch), 0)
    cj = lax.broadcasted_iota(jnp.int32, (ch, ch), 1)
    causal = cj <= ci
    n_chunks = ts // ch
    heads = range(GLA_HEADS)

    def k_lanes(hd):
        return pl.ds(hd * GLA_DK_HEAD, GLA_DK_HEAD)

    def v_lanes(hd):
        return pl.ds(hd * GLA_DV_HEAD, GLA_DV_HEAD)

    def intra_scores(c):
        rows = pl.ds(c * ch, ch)
        out = []
        for hd in heads:
            bc = b_scr[rows, k_lanes(hd)]
            mid = bc[ch // 2:ch // 2 + 1, :]
            qe = (q_scr[rows, k_lanes(hd)] * jnp.exp(bc - mid)).astype(BF16)
            ke = (k_scr[rows, k_lanes(hd)] * jnp.exp(mid - bc)).astype(BF16)
            out.append(_dot_nt(qe, ke))
        return out

    a_next = intra_scores(0)
    for c in range(n_chunks):
        rows = pl.ds(c * ch, ch)
        a_cur = a_next
        if c + 1 < n_chunks:
            a_next = intra_scores(c + 1)
        for hd in heads:
            bc = b_scr[rows, k_lanes(hd)]
            last = bc[ch - 1:ch, :]
            qc = q_scr[rows, k_lanes(hd)]
            kc = k_scr[rows, k_lanes(hd)]
            vcb = v_scr[rows, v_lanes(hd)].astype(BF16)
            st = state_ref[hd]
            kdt = (kc * jnp.exp(last - bc)).T.astype(BF16)
            a = jnp.where(causal, a_cur[hd], 0.0).astype(BF16)
            res = _dot(jnp.concatenate([a, kdt], axis=0), vcb)
            inter = _dot((qc * jnp.exp(bc)).astype(BF16), st.astype(BF16))
            decay = jnp.broadcast_to(jnp.exp(last), (GLA_DK_HEAD, GLA_DK_HEAD)).T
            decay = jnp.concatenate([decay] * (GLA_DV_HEAD // GLA_DK_HEAD), axis=1)
            state_ref[hd] = st * decay + res[ch:]
            o_scr[rows, v_lanes(hd)] = res[:ch] + inter

    for i in range(n_sub):
        rows = pl.ds(i * sub, sub)
        r = _dot(hb_scr[rows, :], wr_ref[...])
        parts = []
        for hd in range(GLA_HEADS):
            oh = o_scr[rows, pl.ds(hd * GLA_DV_HEAD, GLA_DV_HEAD)]
            yh = oh * lax.rsqrt(jnp.mean(oh * oh, axis=-1, keepdims=True) + NORM_EPS)
            parts.append(yh * hng_ref[...])
        og = (jnp.concatenate(parts, axis=-1) * _silu(r)).astype(BF16)
        x_new = x_ref[rows, :] + gate * _dot(og, wo_ref[...])
        out_ref[rows, :] = x_new
        h_next = _modulated_rmsnorm(x_new, next_ng_ref[...], next_mod_ref[0:1, :], next_mod_ref[1:2, :])
        _store_group_orders(h_next, i * sub, h_refs, hf_scr, tmp_scr)


def _gla_layer(x, mod, norm_g, w_in, w_alpha, b_alpha, head_norm_g, w_out, next_mod, next_norm_g):
    batch, seq, _ = x.shape
    ts = GLA_ROW_TILE
    o_q, o_k, o_v, o_g, o_r = 0, GLA_DK, 2 * GLA_DK, 2 * GLA_DK + GLA_DV, 2 * GLA_DK + GLA_DV + GLA_RANK
    w_in_b = w_in.astype(BF16)
    wq, wk, wv = w_in_b[:, o_q:o_k], w_in_b[:, o_k:o_v], w_in_b[:, o_v:o_g]
    wr = w_in_b[:, o_r:]
    wg = jnp.pad(w_in_b[:, o_g:o_r], ((0, 0), (0, LANES - GLA_RANK)))
    wa = jnp.pad(w_alpha, ((0, LANES - GLA_RANK), (0, 0))).astype(BF16)

    def const(shape):
        return pl.BlockSpec(shape, lambda b, t: (0,) * len(shape))

    return pl.pallas_call(
        _gla_kernel,
        grid=(batch, seq // ts),
        in_specs=[
            pl.BlockSpec((None, ts, D_MODEL), lambda b, t: (b, t, 0)),
            pl.BlockSpec((None, 3, D_MODEL), lambda b, t: (b, 0, 0)),
            const((1, D_MODEL)),
            const((D_MODEL, GLA_DK)), const((D_MODEL, GLA_DK)), const((D_MODEL, GLA_DV)),
            const((D_MODEL, GLA_DV)), const((D_MODEL, LANES)),
            const((LANES, GLA_DK)), const((1, GLA_DK)),
            const((1, GLA_DV_HEAD)), const((GLA_DV, D_MODEL)),
            pl.BlockSpec((None, 3, D_MODEL), lambda b, t: (b, 0, 0)),
            const((1, D_MODEL)),
        ],
        out_specs=[pl.BlockSpec((None, ts, D_MODEL), lambda b, t: (b, t, 0))] + [
            pl.BlockSpec((None, dil, ts // dil, D_MODEL), lambda b, t: (b, 0, t, 0)) for _, dil in DSW_GROUPS],
        out_shape=[jax.ShapeDtypeStruct(x.shape, F32)] + [
            jax.ShapeDtypeStruct((batch, dil, seq // dil, D_MODEL), BF16) for _, dil in DSW_GROUPS],
        scratch_shapes=[
            pltpu.VMEM((GLA_HEADS, GLA_DK_HEAD, GLA_DV_HEAD), F32),
            pltpu.VMEM((ts, GLA_DK), F32), pltpu.VMEM((ts, GLA_DK), F32),
            pltpu.VMEM((ts, GLA_DV), F32), pltpu.VMEM((ts, GLA_DK), F32),
            pltpu.VMEM((ts, GLA_DV), F32),
            pltpu.VMEM((ts, D_MODEL), BF16),
            pltpu.VMEM((D_MODEL // LANES, GLA_SUB_TILE, LANES), F32),
            pltpu.VMEM((GLA_SUB_TILE, LANES), F32),
        ],
        compiler_params=pltpu.CompilerParams(
            dimension_semantics=("arbitrary", "arbitrary"), vmem_limit_bytes=VMEM_LIMIT_BYTES),
        name="gla_layer",
    )(x, mod, norm_g.reshape(1, D_MODEL), wq, wk, wv, wr, wg, wa, b_alpha.reshape(1, GLA_DK),
      head_norm_g.reshape(1, GLA_DV_HEAD), w_out.astype(BF16), next_mod, next_norm_g.reshape(1, D_MODEL))


def _store_group_orders(h, row0, out_refs, hf_scr, tmp_scr):
    n_rows = h.shape[0]
    out_refs[0][0, pl.ds(row0, n_rows), :] = h.astype(BF16)
    for cb in range(D_MODEL // LANES):
        hf_scr[cb] = h[:, cb * LANES:(cb + 1) * LANES]
    for cb in range(D_MODEL // LANES):
        lanes = pl.ds(cb * LANES, LANES)
        src = hf_scr.at[cb]
        for gi in range(1, N_GROUPS):
            prev_dil = DSW_GROUPS[gi - 1][1]
            prev_len = n_rows // prev_dil
            seg_len = prev_len // REGATHER_STRIDE
            for p in range(prev_dil):
                for r2 in range(REGATHER_STRIDE):
                    seg = p * REGATHER_STRIDE + r2
                    piece = src[pl.ds(p * prev_len + r2, seg_len, stride=REGATHER_STRIDE), :]
                    out_refs[gi][seg, pl.ds(row0 // DSW_GROUPS[gi][1], seg_len), lanes] = piece.astype(BF16)
                    if gi + 1 < N_GROUPS:
                        tmp_scr[pl.ds(seg * seg_len, seg_len), :] = piece
            src = tmp_scr


def _dsw_proj_kernel(h_ref, w_ref, out_ref):
    out_scale = jnp.where(pl.program_id(1) == 0, QUERY_SCALE, 1.0).astype(F32)
    w = w_ref[...].astype(BF16)
    for s in range(h_ref.shape[0] // PROJ_ROW_CHUNK):
        rows = pl.ds(s * PROJ_ROW_CHUNK, PROJ_ROW_CHUNK)
        out_ref[rows, :] = (_dot(h_ref[rows, :], w) * out_scale).astype(BF16)


def _dsw_group_proj(h, w_in, sections):
    m_rows = h.shape[0]
    n_sec = len(sections)
    tm = PROJ_ROW_TILE

    def w_block(i, j):
        col = functools.reduce(lambda a, b: a + b, [jnp.where(j == k, sec, 0) for k, sec in enumerate(sections)])
        return (0, 0, col)

    return pl.pallas_call(
        _dsw_proj_kernel,
        grid=(m_rows // tm, n_sec),
        in_specs=[
            pl.BlockSpec((tm, D_MODEL), lambda i, j: (i, 0)),
            pl.BlockSpec((None, D_MODEL, DSW_WIDTH), w_block),
        ],
        out_specs=pl.BlockSpec((None, tm, DSW_WIDTH), lambda i, j: (j, i, 0)),
        out_shape=jax.ShapeDtypeStruct((n_sec, m_rows, DSW_WIDTH), BF16),
        compiler_params=pltpu.CompilerParams(
            dimension_semantics=("arbitrary", "arbitrary"), vmem_limit_bytes=VMEM_LIMIT_BYTES),
        name="dsw_group_proj",
    )(h, w_in)


def _t5_causal_bucket(n):
    max_exact = REL_BUCKETS // 2
    nf = np.maximum(n, 1).astype(np.float32)
    large = max_exact + (np.log(nf / max_exact) / math.log(REL_MAX_DIST / max_exact)
                         * (REL_BUCKETS - max_exact)).astype(np.int32)
    large = np.minimum(large, REL_BUCKETS - 1)
    return np.where(n < max_exact, n, large).astype(np.int32)


def _bucket_tables():
    qi = np.arange(DSW_SPAN)[:, None]
    kj = np.arange(2 * DSW_SPAN)[None, :]
    steps = qi + DSW_SPAN - kj
    in_window = (steps >= 0) & (steps <= DSW_SPAN)
    tables = []
    for _, dil in DSW_GROUPS:
        bucket = _t5_causal_bucket(np.clip(steps, 0, DSW_SPAN) * dil)
        tables.append(np.where(in_window, bucket, -1))
    return np.stack(tables).astype(np.int32)


def _dsw_bias_kernel(rel_ref, bucket_ref, out_ref):
    bucket = bucket_ref[...]
    masks = [bucket == bk for bk in range(REL_BUCKETS)]
    outside = jnp.where(bucket < 0, -jnp.inf, 0.0).astype(F32)
    for h in range(DSW_HEADS):
        acc = outside
        for bk in range(REL_BUCKETS):
            acc = jnp.where(masks[bk], rel_ref[bk, h] * LOG2E, acc)
        out_ref[pl.ds(h * DSW_SPAN, DSW_SPAN), :] = acc


def _dsw_bias(rel_bias):
    return pl.pallas_call(
        _dsw_bias_kernel,
        grid=(N_GROUPS,),
        in_specs=[
            pl.BlockSpec(memory_space=pltpu.SMEM),
            pl.BlockSpec((None, DSW_SPAN, 2 * DSW_SPAN), lambda g: (g, 0, 0)),
        ],
        out_specs=pl.BlockSpec((None, DSW_HEADS * DSW_SPAN, 2 * DSW_SPAN), lambda g: (g, 0, 0)),
        out_shape=jax.ShapeDtypeStruct((N_GROUPS, DSW_HEADS * DSW_SPAN, 2 * DSW_SPAN), F32),
        compiler_params=pltpu.CompilerParams(dimension_semantics=("arbitrary",)),
        name="dsw_bias",
    )(rel_bias, jnp.asarray(_bucket_tables()))


def _dsw_attention_kernel(*refs):
    qkv_refs = refs[:3 * N_GROUPS]
    gate_ref, bias_ref, out_ref, pv_scr, m_scr, l_scr = refs[3 * N_GROUPS:]
    n_batch, seq, _ = out_ref.shape
    span = DSW_SPAN
    lane = lax.broadcasted_iota(jnp.int32, (span, LANES), 1)
    head_masks = [(lane // DSW_HEAD_DIM) == hh for hh in range(HEADS_PER_STEP)]

    def is_first(gi, blk):
        return blk % (seq // DSW_GROUPS[gi][1] // span) == 0

    def key_rows(gi, blk):
        return pl.ds(blk * span, span) if is_first(gi, blk) else pl.ds((blk - 1) * span, 2 * span)

    def attend(bb):
        def scores(gi, blk):
            q = qkv_refs[3 * gi][bb, pl.ds(blk * span, span), :]
            q2 = jnp.concatenate([jnp.where(mk, q, jnp.zeros_like(q)) for mk in head_masks], axis=0)
            return _dot_nt(q2, qkv_refs[3 * gi + 1][bb, key_rows(gi, blk), :])

        def biased_max(gi, blk, s):
            if is_first(gi, blk):
                s = s + bias_ref[gi, :, pl.ds(span, span)]
            else:
                s = s + bias_ref[gi]
            return s, jnp.max(s, axis=-1, keepdims=True)

        units = [(gi, blk) for gi in range(N_GROUPS) for blk in range(seq // span)]
        raw = {i: scores(*units[i]) for i in range(2)}
        staged = {0: biased_max(*units[0], raw.pop(0))}
        for idx, (gi, blk) in enumerate(units):
            dil = DSW_GROUPS[gi][1]
            if idx + 2 < len(units):
                raw[idx + 2] = scores(*units[idx + 2])
            if idx + 1 < len(units):
                staged[idx + 1] = biased_max(*units[idx + 1], raw.pop(idx + 1))
            s, m = staged.pop(idx)
            e = jnp.exp2(s - m)
            l = jnp.sum(e, axis=-1, keepdims=True)
            pv = _dot(e.astype(BF16), qkv_refs[3 * gi + 2][bb, key_rows(gi, blk), :])
            blocks_per_seg = seq // dil // span
            residue = SEG_RESIDUE[gi][blk // blocks_per_seg]
            i0 = (blk % blocks_per_seg) * span
            dst = pl.ds(i0 * dil + residue, span, stride=dil) if dil > 1 else pl.ds(blk * span, span)
            pv_scr[gi, dst, :] = jnp.where(head_masks[0], pv[:span], pv[span:])
            m_scr[gi, dst, :] = jnp.where(head_masks[0], m[:span], m[span:])
            l_scr[gi, dst, :] = jnp.where(head_masks[0], l[:span], l[span:])

        tops = [m_scr[g] for g in range(N_GROUPS)]
        top = functools.reduce(jnp.maximum, tops)
        wts = [jnp.exp2(v - top) for v in tops]
        num = functools.reduce(lambda a, b: a + b, [wts[g] * pv_scr[g] for g in range(N_GROUPS)])
        den = functools.reduce(lambda a, b: a + b, [wts[g] * l_scr[g] for g in range(N_GROUPS)])
        out_ref[bb] = (num * _silu(gate_ref[bb].astype(F32)) / den).astype(BF16)

    def body(bb, carry):
        attend(bb)
        return carry

    lax.fori_loop(0, n_batch, body, 0)


def _dsw_attention(projs, bias, batch):
    seq = projs[0].shape[1] // batch
    n_pairs = DSW_HEADS // HEADS_PER_STEP
    bps = ATT_BATCH_PER_STEP
    operands, in_specs = [], []

    def add(arr, j):
        operands.append(arr.reshape(arr.shape[0], batch, seq, DSW_WIDTH))
        in_specs.append(pl.BlockSpec((None, bps, seq, LANES), lambda b, hp, j=j: (j, b, 0, hp)))

    for gi in range(N_GROUPS):
        for j in range(3):
            add(projs[gi], j)
    add(projs[0], 3)
    in_specs.append(
        pl.BlockSpec((N_GROUPS, HEADS_PER_STEP * DSW_SPAN, 2 * DSW_SPAN), lambda b, hp: (0, hp, 0)))
    return pl.pallas_call(
        _dsw_attention_kernel,
        grid=(batch // bps, n_pairs),
        in_specs=in_specs,
        out_specs=pl.BlockSpec((bps, seq, LANES), lambda b, hp: (b, 0, hp)),
        out_shape=jax.ShapeDtypeStruct((batch, seq, DSW_WIDTH), BF16),
        scratch_shapes=[pltpu.VMEM((N_GROUPS, seq, LANES), F32)] * 3,
        compiler_params=pltpu.CompilerParams(
            dimension_semantics=("arbitrary", "arbitrary"), vmem_limit_bytes=VMEM_LIMIT_BYTES),
        name="dsw_attention",
    )(*operands, bias)


def _dsw_out_proj_kernel(o_ref, x_ref, mod_ref, w_ref, fg_ref, out_ref):
    gate = mod_ref[2:3, :]
    w = w_ref[...].astype(BF16)
    for s in range(o_ref.shape[0] // PROJ_ROW_CHUNK):
        rows = pl.ds(s * PROJ_ROW_CHUNK, PROJ_ROW_CHUNK)
        xn = x_ref[rows, :] + gate * _dot(o_ref[rows, :], w)
        out_ref[rows, :] = xn * lax.rsqrt(jnp.mean(xn * xn, axis=-1, keepdims=True) + NORM_EPS) * fg_ref[...]


def _dsw_out_proj(og, x, mod, w_out, final_g):
    batch, seq, _ = x.shape
    ts = OUT_ROW_TILE
    return pl.pallas_call(
        _dsw_out_proj_kernel,
        grid=(batch, seq // ts),
        in_specs=[
            pl.BlockSpec((None, ts, DSW_WIDTH), lambda b, t: (b, t, 0)),
            pl.BlockSpec((None, ts, D_MODEL), lambda b, t: (b, t, 0)),
            pl.BlockSpec((None, 3, D_MODEL), lambda b, t: (b, 0, 0)),
            pl.BlockSpec((DSW_WIDTH, D_MODEL), lambda b, t: (0, 0)),
            pl.BlockSpec((1, D_MODEL), lambda b, t: (0, 0)),
        ],
        out_specs=pl.BlockSpec((None, ts, D_MODEL), lambda b, t: (b, t, 0)),
        out_shape=jax.ShapeDtypeStruct(x.shape, F32),
        compiler_params=pltpu.CompilerParams(
            dimension_semantics=("arbitrary", "arbitrary"), vmem_limit_bytes=VMEM_LIMIT_BYTES),
        name="dsw_out_proj",
    )(og, x, mod, w_out, final_g.reshape(1, D_MODEL))


def kernel(x, c, ada_w, ada_b, norm_g, gla_w_in, gla_w_alpha, gla_b_alpha, gla_norm_g, gla_w_out,
           dsw_w_in, dsw_w_out, rel_bias, final_g):
    batch = x.shape[0]
    mod = _adaln_mod(c, ada_w, ada_b)
    mod = mod.reshape(DEPTH, batch, 3, D_MODEL)
    x, *h_orders = _gla_layer(x, mod[0], norm_g[0], gla_w_in[0], gla_w_alpha[0], gla_b_alpha[0],
                              gla_norm_g[0], gla_w_out[0], mod[1], norm_g[1])
    projs = []
    for gi in range(N_GROUPS):
        sections = [3 * gi, 3 * gi + 1, 3 * gi + 2]
        if gi == 0:
            sections.append(3 * N_GROUPS)
        projs.append(_dsw_group_proj(h_orders[gi].reshape(-1, D_MODEL), dsw_w_in, sections))
    bias = _dsw_bias(rel_bias)
    og = _dsw_attention(projs, bias, batch)
    return _dsw_out_proj(og, x, mod[1], dsw_w_out[0], final_g)
```

```python
import functools
import math

import jax
import jax.numpy as jnp
import numpy as np
from jax import lax
from jax.experimental import pallas as pl
from jax.experimental.pallas import tpu as pltpu

F32 = jnp.float32
BF16 = jnp.bfloat16

D_MODEL = 1024
DEPTH = 2
NORM_EPS = 1e-6

GLA_HEADS = 4
GLA_DK = D_MODEL // 2
GLA_DV = D_MODEL
GLA_DK_HEAD = GLA_DK // GLA_HEADS
GLA_DV_HEAD = GLA_DV // GLA_HEADS
GLA_RANK = 16
GLA_TAU = 16.0
GLA_CHUNK = 64

DSW_HEADS = 16
DSW_HEAD_DIM = D_MODEL // DSW_HEADS
DSW_WIDTH = DSW_HEADS * DSW_HEAD_DIM
DSW_GROUPS = ((128, 1), (512, 4), (2048, 16))
N_GROUPS = len(DSW_GROUPS)
DSW_SPAN = 128
REGATHER_STRIDE = 4
assert N_GROUPS == 3 and all(b[1] == a[1] * REGATHER_STRIDE for a, b in zip(DSW_GROUPS, DSW_GROUPS[1:]))


def _segment_residues():
    table = [[0]]
    for gi in range(1, N_GROUPS):
        prev_dil = DSW_GROUPS[gi - 1][1]
        table.append([res + prev_dil * r2 for res in table[-1] for r2 in range(REGATHER_STRIDE)])
    return table


SEG_RESIDUE = _segment_residues()
REL_BUCKETS = 32
REL_MAX_DIST = 2048

LANES = 128
HEADS_PER_STEP = LANES // DSW_HEAD_DIM
assert HEADS_PER_STEP == 2
ATT_BATCH_PER_STEP = 2
LOG2E = math.log2(math.e)
QUERY_SCALE = DSW_HEAD_DIM ** -0.5 * LOG2E
VMEM_LIMIT_BYTES = 56 * 1024 * 1024

GLA_ROW_TILE = 512
OUT_ROW_TILE = 2048
PROJ_ROW_CHUNK = 512
PROJ_ROW_TILE = 4096

_NT = (((1,), (1,)), ((), ()))


def _dot(a, b, precision=None):
    return jnp.dot(a, b, preferred_element_type=F32, precision=precision)


def _dot_nt(a, b):
    return lax.dot_general(a, b, _NT, preferred_element_type=F32)


def _silu(v):
    return v * (1.0 / (1.0 + jnp.exp(-v)))


def _log_sigmoid(z):
    return jnp.minimum(z, 0.0) - jnp.log(1.0 + jnp.exp(-jnp.abs(z)))


def _chunk_cumsum(x, chunk):
    pos = lax.broadcasted_iota(jnp.int32, x.shape, 0) % chunk
    step = 1
    while step < chunk:
        x = x + jnp.where(pos >= step, pltpu.roll(x, shift=step, axis=0), 0.0)
        step *= 2
    return x


def _modulated_rmsnorm(x, g, shift, scale):
    y = x * lax.rsqrt(jnp.mean(x * x, axis=-1, keepdims=True) + NORM_EPS)
    return y * (g * (1.0 + scale)) + shift


def _adaln_kernel(c_ref, w_ref, b_ref, out_ref):
    c_act = _silu(c_ref[...])
    out_ref[...] = _dot(c_act.astype(BF16), w_ref[...].astype(BF16)) + b_ref[...]


def _adaln_mod(c, ada_w, ada_b):
    batch = c.shape[0]
    n_col = 3
    return pl.pallas_call(
        _adaln_kernel,
        grid=(DEPTH, n_col),
        in_specs=[
            pl.BlockSpec((batch, D_MODEL), lambda i, j: (0, 0)),
            pl.BlockSpec((None, D_MODEL, D_MODEL), lambda i, j: (i, 0, j)),
            pl.BlockSpec((None, 1, D_MODEL), lambda i, j: (i, 0, j)),
        ],
        out_specs=pl.BlockSpec((None, batch, D_MODEL), lambda i, j: (i, 0, j)),
        out_shape=jax.ShapeDtypeStruct((DEPTH, batch, 3 * D_MODEL), F32),
        compiler_params=pltpu.CompilerParams(
            dimension_semantics=("arbitrary", "arbitrary"), vmem_limit_bytes=VMEM_LIMIT_BYTES),
        name="adaln_mod",
    )(c, ada_w, ada_b.reshape(DEPTH, 1, 3 * D_MODEL))


def _gla_kernel(x_ref, mod_ref, ng_ref, wq_ref, wk_ref, wv_ref, wr_ref, wg_ref, wa_ref, ba_ref,
                hng_ref, wo_ref, next_mod_ref, next_ng_ref, out_ref, *rest):
    h_refs = rest[:N_GROUPS]
    state_ref, q_scr, k_scr, v_scr, b_scr, o_scr, hb_scr, hf_scr, tmp_scr = rest[N_GROUPS:]
    ts = GLA_ROW_TILE
    ch = GLA_CHUNK

    @pl.when(pl.program_id(1) == 0)
    def _():
        state_ref[...] = jnp.zeros_like(state_ref)

    shift, scale, gate = mod_ref[0:1, :], mod_ref[1:2, :], mod_ref[2:3, :]
    hb_scr[...] = _modulated_rmsnorm(x_ref[...], ng_ref[...], shift, scale).astype(BF16)
    g_lr = _dot(hb_scr[...], wg_ref[...])
    z = _dot(g_lr.astype(BF16), wa_ref[...]) + ba_ref[...]
    b_scr[...] = _chunk_cumsum(_log_sigmoid(z) / GLA_TAU, ch)
    q_scr[...] = _dot(hb_scr[...], wq_ref[...]) * (GLA_DK_HEAD ** -0.5)
    k_scr[...] = _dot(hb_scr[...], wk_ref[...])
    v_scr[...] = _dot(hb_scr[...], wv_ref[...])

    ci = lax.broadcasted_iota(jnp.int32, (ch, ch), 0)
    cj = lax.broadcasted_iota(jnp.int32, (ch, ch), 1)
    causal = cj <= ci
    n_chunks = ts // ch
    heads = range(GLA_HEADS)

    def k_lanes(hd):
        return pl.ds(hd * GLA_DK_HEAD, GLA_DK_HEAD)

    def v_lanes(hd):
        return pl.ds(hd * GLA_DV_HEAD, GLA_DV_HEAD)

    def intra_scores(c):
        rows = pl.ds(c * ch, ch)
        out = []
        for hd in heads:
            bc = b_scr[rows, k_lanes(hd)]
            mid = bc[ch // 2:ch // 2 + 1, :]
            qe = (q_scr[rows, k_lanes(hd)] * jnp.exp(bc - mid)).astype(BF16)
            ke = (k_scr[rows, k_lanes(hd)] * jnp.exp(mid - bc)).astype(BF16)
            out.append(_dot_nt(qe, ke))
        return out

    a_next = intra_scores(0)
    for c in range(n_chunks):
        rows = pl.ds(c * ch, ch)
        a_cur = a_next
        if c + 1 < n_chunks:
            a_next = intra_scores(c + 1)
        for hd in heads:
            bc = b_scr[rows, k_lanes(hd)]
            last = bc[ch - 1:ch, :]
            qc = q_scr[rows, k_lanes(hd)]
            kc = k_scr[rows, k_lanes(hd)]
            vcb = v_scr[rows, v_lanes(hd)].astype(BF16)
            st = state_ref[hd]
            kdt = (kc * jnp.exp(last - bc)).T.astype(BF16)
            a = jnp.where(causal, a_cur[hd], 0.0).astype(BF16)
            res = _dot(jnp.concatenate([a, kdt], axis=0), vcb)
            inter = _dot((qc * jnp.exp(bc)).astype(BF16), st.astype(BF16))
            decay = jnp.broadcast_to(jnp.exp(last), (GLA_DK_HEAD, GLA_DK_HEAD)).T
            decay = jnp.concatenate([decay] * (GLA_DV_HEAD // GLA_DK_HEAD), axis=1)
            state_ref[hd] = st * decay + res[ch:]
            o_scr[rows, v_lanes(hd)] = res[:ch] + inter

    r = _dot(hb_scr[...], wr_ref[...])
    parts = []
    for hd in range(GLA_HEADS):
        oh = o_scr[:, pl.ds(hd * GLA_DV_HEAD, GLA_DV_HEAD)]
        yh = oh * lax.rsqrt(jnp.mean(oh * oh, axis=-1, keepdims=True) + NORM_EPS)
        parts.append(yh * hng_ref[...])
    og = (jnp.concatenate(parts, axis=-1) * _silu(r)).astype(BF16)
    x_new = x_ref[...] + gate * _dot(og, wo_ref[...])
    out_ref[...] = x_new
    h_next = _modulated_rmsnorm(x_new, next_ng_ref[...], next_mod_ref[0:1, :], next_mod_ref[1:2, :])
    _store_group_orders(h_next, h_refs, hf_scr, tmp_scr)


def _gla_layer(x, mod, norm_g, w_in, w_alpha, b_alpha, head_norm_g, w_out, next_mod, next_norm_g):
    batch, seq, _ = x.shape
    ts = GLA_ROW_TILE
    o_q, o_k, o_v, o_g, o_r = 0, GLA_DK, 2 * GLA_DK, 2 * GLA_DK + GLA_DV, 2 * GLA_DK + GLA_DV + GLA_RANK
    w_in_b = w_in.astype(BF16)
    wq, wk, wv = w_in_b[:, o_q:o_k], w_in_b[:, o_k:o_v], w_in_b[:, o_v:o_g]
    wr = w_in_b[:, o_r:]
    wg = jnp.pad(w_in_b[:, o_g:o_r], ((0, 0), (0, LANES - GLA_RANK)))
    wa = jnp.pad(w_alpha, ((0, LANES - GLA_RANK), (0, 0))).astype(BF16)

    def const(shape):
        return pl.BlockSpec(shape, lambda b, t: (0,) * len(shape))

    return pl.pallas_call(
        _gla_kernel,
        grid=(batch, seq // ts),
        in_specs=[
            pl.BlockSpec((None, ts, D_MODEL), lambda b, t: (b, t, 0)),
            pl.BlockSpec((None, 3, D_MODEL), lambda b, t: (b, 0, 0)),
            const((1, D_MODEL)),
            const((D_MODEL, GLA_DK)), const((D_MODEL, GLA_DK)), const((D_MODEL, GLA_DV)),
            const((D_MODEL, GLA_DV)), const((D_MODEL, LANES)),
            const((LANES, GLA_DK)), const((1, GLA_DK)),
            const((1, GLA_DV_HEAD)), const((GLA_DV, D_MODEL)),
            pl.BlockSpec((None, 3, D_MODEL), lambda b, t: (b, 0, 0)),
            const((1, D_MODEL)),
        ],
        out_specs=[pl.BlockSpec((None, ts, D_MODEL), lambda b, t: (b, t, 0))] + [
            pl.BlockSpec((None, dil, ts // dil, D_MODEL), lambda b, t: (b, 0, t, 0)) for _, dil in DSW_GROUPS],
        out_shape=[jax.ShapeDtypeStruct(x.shape, F32)] + [
            jax.ShapeDtypeStruct((batch, dil, seq // dil, D_MODEL), BF16) for _, dil in DSW_GROUPS],
        scratch_shapes=[
            pltpu.VMEM((GLA_HEADS, GLA_DK_HEAD, GLA_DV_HEAD), F32),
            pltpu.VMEM((ts, GLA_DK), F32), pltpu.VMEM((ts, GLA_DK), F32),
            pltpu.VMEM((ts, GLA_DV), F32), pltpu.VMEM((ts, GLA_DK), F32),
            pltpu.VMEM((ts, GLA_DV), F32),
            pltpu.VMEM((ts, D_MODEL), BF16),
            pltpu.VMEM((D_MODEL // LANES, ts, LANES), F32),
            pltpu.VMEM((ts, LANES), F32),
        ],
        compiler_params=pltpu.CompilerParams(
            dimension_semantics=("arbitrary", "arbitrary"), vmem_limit_bytes=VMEM_LIMIT_BYTES),
        name="gla_layer",
    )(x, mod, norm_g.reshape(1, D_MODEL), wq, wk, wv, wr, wg, wa, b_alpha.reshape(1, GLA_DK),
      head_norm_g.reshape(1, GLA_DV_HEAD), w_out.astype(BF16), next_mod, next_norm_g.reshape(1, D_MODEL))


def _store_group_orders(h, out_refs, hf_scr, tmp_scr):
    n_rows = h.shape[0]
    out_refs[0][0] = h.astype(BF16)
    for cb in range(D_MODEL // LANES):
        hf_scr[cb] = h[:, cb * LANES:(cb + 1) * LANES]
    for cb in range(D_MODEL // LANES):
        lanes = pl.ds(cb * LANES, LANES)
        src = hf_scr.at[cb]
        for gi in range(1, N_GROUPS):
            prev_dil = DSW_GROUPS[gi - 1][1]
            prev_len = n_rows // prev_dil
            seg_len = prev_len // REGATHER_STRIDE
            for p in range(prev_dil):
                for r2 in range(REGATHER_STRIDE):
                    seg = p * REGATHER_STRIDE + r2
                    piece = src[pl.ds(p * prev_len + r2, seg_len, stride=REGATHER_STRIDE), :]
                    out_refs[gi][seg, :, lanes] = piece.astype(BF16)
                    if gi + 1 < N_GROUPS:
                        tmp_scr[pl.ds(seg * seg_len, seg_len), :] = piece
            src = tmp_scr


def _dsw_proj_kernel(h_ref, w_ref, out_ref):
    out_scale = jnp.where(pl.program_id(1) == 0, QUERY_SCALE, 1.0).astype(F32)
    w = w_ref[...].astype(BF16)
    for s in range(h_ref.shape[0] // PROJ_ROW_CHUNK):
        rows = pl.ds(s * PROJ_ROW_CHUNK, PROJ_ROW_CHUNK)
        out_ref[rows, :] = (_dot(h_ref[rows, :], w) * out_scale).astype(BF16)


def _dsw_group_proj(h, w_in, sections):
    m_rows = h.shape[0]
    n_sec = len(sections)
    tm = PROJ_ROW_TILE

    def w_block(i, j):
        col = functools.reduce(lambda a, b: a + b, [jnp.where(j == k, sec, 0) for k, sec in enumerate(sections)])
        return (0, 0, col)

    return pl.pallas_call(
        _dsw_proj_kernel,
        grid=(m_rows // tm, n_sec),
        in_specs=[
            pl.BlockSpec((tm, D_MODEL), lambda i, j: (i, 0)),
            pl.BlockSpec((None, D_MODEL, DSW_WIDTH), w_block),
        ],
        out_specs=pl.BlockSpec((None, tm, DSW_WIDTH), lambda i, j: (j, i, 0)),
        out_shape=jax.ShapeDtypeStruct((n_sec, m_rows, DSW_WIDTH), BF16),
        compiler_params=pltpu.CompilerParams(
            dimension_semantics=("arbitrary", "arbitrary"), vmem_limit_bytes=VMEM_LIMIT_BYTES),
        name="dsw_group_proj",
    )(h, w_in)


def _t5_causal_bucket(n):
    max_exact = REL_BUCKETS // 2
    nf = np.maximum(n, 1).astype(np.float32)
    large = max_exact + (np.log(nf / max_exact) / math.log(REL_MAX_DIST / max_exact)
                         * (REL_BUCKETS - max_exact)).astype(np.int32)
    large = np.minimum(large, REL_BUCKETS - 1)
    return np.where(n < max_exact, n, large).astype(np.int32)


def _bucket_tables():
    qi = np.arange(DSW_SPAN)[:, None]
    kj = np.arange(2 * DSW_SPAN)[None, :]
    steps = qi + DSW_SPAN - kj
    in_window = (steps >= 0) & (steps <= DSW_SPAN)
    tables = []
    for _, dil in DSW_GROUPS:
        bucket = _t5_causal_bucket(np.clip(steps, 0, DSW_SPAN) * dil)
        tables.append(np.where(in_window, bucket, -1))
    return np.stack(tables).astype(np.int32)


def _dsw_bias_kernel(rel_ref, bucket_ref, out_ref):
    bucket = bucket_ref[...]
    masks = [bucket == bk for bk in range(REL_BUCKETS)]
    outside = jnp.where(bucket < 0, -jnp.inf, 0.0).astype(F32)
    for h in range(DSW_HEADS):
        row = outside
        for bk in range(REL_BUCKETS):
            row = jnp.where(masks[bk], rel_ref[bk, h] * LOG2E, row)
        tile = jnp.broadcast_to(row, (DSW_SPAN, 2 * DSW_SPAN))
        out_ref[pl.ds(h * DSW_SPAN, DSW_SPAN), :] = pltpu.roll(tile, shift=0, axis=1, stride=1, stride_axis=0)


def _dsw_bias(rel_bias):
    return pl.pallas_call(
        _dsw_bias_kernel,
        grid=(N_GROUPS,),
        in_specs=[
            pl.BlockSpec(memory_space=pltpu.SMEM),
            pl.BlockSpec((None, 1, 2 * DSW_SPAN), lambda g: (g, 0, 0)),
        ],
        out_specs=pl.BlockSpec((None, DSW_HEADS * DSW_SPAN, 2 * DSW_SPAN), lambda g: (g, 0, 0)),
        out_shape=jax.ShapeDtypeStruct((N_GROUPS, DSW_HEADS * DSW_SPAN, 2 * DSW_SPAN), F32),
        compiler_params=pltpu.CompilerParams(dimension_semantics=("arbitrary",)),
        name="dsw_bias",
    )(rel_bias, jnp.asarray(_bucket_tables()[:, :1, :]))


def _dsw_attention_kernel(*refs):
    qkv_refs = refs[:3 * N_GROUPS]
    gate_ref, bias_ref, out_ref, pv_scr, m_scr, l_scr = refs[3 * N_GROUPS:]
    n_batch, seq, _ = out_ref.shape
    span = DSW_SPAN
    lane = lax.broadcasted_iota(jnp.int32, (span, LANES), 1)
    head_masks = [(lane // DSW_HEAD_DIM) == hh for hh in range(HEADS_PER_STEP)]

    def is_first(gi, blk):
        return blk % (seq // DSW_GROUPS[gi][1] // span) == 0

    def key_rows(gi, blk):
        return pl.ds(blk * span, span) if is_first(gi, blk) else pl.ds((blk - 1) * span, 2 * span)

    def attend(bb):
        def scores(gi, blk):
            q = qkv_refs[3 * gi][bb, pl.ds(blk * span, span), :]
            q2 = jnp.concatenate([jnp.where(mk, q, jnp.zeros_like(q)) for mk in head_masks], axis=0)
            return _dot_nt(q2, qkv_refs[3 * gi + 1][bb, key_rows(gi, blk), :])

        def biased_max(gi, blk, s):
            if is_first(gi, blk):
                s = s + bias_ref[gi, :, pl.ds(span, span)]
            else:
                s = s + bias_ref[gi]
            return s, jnp.max(s, axis=-1, keepdims=True)

        units = [(gi, blk) for gi in range(N_GROUPS) for blk in range(seq // span)]
        raw = {i: scores(*units[i]) for i in range(2)}
        staged = {0: biased_max(*units[0], raw.pop(0))}
        for idx, (gi, blk) in enumerate(units):
            dil = DSW_GROUPS[gi][1]
            if idx + 2 < len(units):
                raw[idx + 2] = scores(*units[idx + 2])
            if idx + 1 < len(units):
                staged[idx + 1] = biased_max(*units[idx + 1], raw.pop(idx + 1))
            s, m = staged.pop(idx)
            e = jnp.exp2(s - m)
            l = jnp.sum(e, axis=-1, keepdims=True)
            pv = _dot(e.astype(BF16), qkv_refs[3 * gi + 2][bb, key_rows(gi, blk), :])
            blocks_per_seg = seq // dil // span
            residue = SEG_RESIDUE[gi][blk // blocks_per_seg]
            i0 = (blk % blocks_per_seg) * span
            dst = pl.ds(i0 * dil + residue, span, stride=dil) if dil > 1 else pl.ds(blk * span, span)
            pv_scr[gi, dst, :] = jnp.where(head_masks[0], pv[:span], pv[span:])
            m_scr[gi, dst, :] = jnp.where(head_masks[0], m[:span], m[span:])
            l_scr[gi, dst, :] = jnp.where(head_masks[0], l[:span], l[span:])

        tops = [m_scr[g] for g in range(N_GROUPS)]
        top = functools.reduce(jnp.maximum, tops)
        wts = [jnp.exp2(v - top) for v in tops]
        num = functools.reduce(lambda a, b: a + b, [wts[g] * pv_scr[g] for g in range(N_GROUPS)])
        den = functools.reduce(lambda a, b: a + b, [wts[g] * l_scr[g] for g in range(N_GROUPS)])
        out_ref[bb] = (num * _silu(gate_ref[bb].astype(F32)) / den).astype(BF16)

    def body(bb, carry):
        attend(bb)
        return carry

    lax.fori_loop(0, n_batch, body, 0)


def _dsw_attention(projs, bias, batch):
    seq = projs[0].shape[1] // batch
    n_pairs = DSW_HEADS // HEADS_PER_STEP
    bps = ATT_BATCH_PER_STEP
    operands, in_specs = [], []

    def add(arr, j):
        operands.append(arr.reshape(arr.shape[0], batch, seq, DSW_WIDTH))
        in_specs.append(pl.BlockSpec((None, bps, seq, LANES), lambda b, hp, j=j: (j, b, 0, hp)))

    for gi in range(N_GROUPS):
        for j in range(3):
            add(projs[gi], j)
    add(projs[0], 3)
    in_specs.append(
        pl.BlockSpec((N_GROUPS, HEADS_PER_STEP * DSW_SPAN, 2 * DSW_SPAN), lambda b, hp: (0, hp, 0)))
    return pl.pallas_call(
        _dsw_attention_kernel,
        grid=(batch // bps, n_pairs),
        in_specs=in_specs,
        out_specs=pl.BlockSpec((bps, seq, LANES), lambda b, hp: (b, 0, hp)),
        out_shape=jax.ShapeDtypeStruct((batch, seq, DSW_WIDTH), BF16),
        scratch_shapes=[pltpu.VMEM((N_GROUPS, seq, LANES), F32)] * 3,
        compiler_params=pltpu.CompilerParams(
            dimension_semantics=("arbitrary", "arbitrary"), vmem_limit_bytes=VMEM_LIMIT_BYTES),
        name="dsw_attention",
    )(*operands, bias)


def _dsw_out_proj_kernel(o_ref, x_ref, mod_ref, w_ref, fg_ref, out_ref):
    gate = mod_ref[2:3, :]
    w = w_ref[...].astype(BF16)
    for s in range(o_ref.shape[0] // PROJ_ROW_CHUNK):
        rows = pl.ds(s * PROJ_ROW_CHUNK, PROJ_ROW_CHUNK)
        xn = x_ref[rows, :] + gate * _dot(o_ref[rows, :], w)
        out_ref[rows, :] = xn * lax.rsqrt(jnp.mean(xn * xn, axis=-1, keepdims=True) + NORM_EPS) * fg_ref[...]


def _dsw_out_proj(og, x, mod, w_out, final_g):
    batch, seq, _ = x.shape
    ts = OUT_ROW_TILE
    return pl.pallas_call(
        _dsw_out_proj_kernel,
        grid=(batch, seq // ts),
        in_specs=[
            pl.BlockSpec((None, ts, DSW_WIDTH), lambda b, t: (b, t, 0)),
            pl.BlockSpec((None, ts, D_MODEL), lambda b, t: (b, t, 0)),
            pl.BlockSpec((None, 3, D_MODEL), lambda b, t: (b, 0, 0)),
            pl.BlockSpec((DSW_WIDTH, D_MODEL), lambda b, t: (0, 0)),
            pl.BlockSpec((1, D_MODEL), lambda b, t: (0, 0)),
        ],
        out_specs=pl.BlockSpec((None, ts, D_MODEL), lambda b, t: (b, t, 0)),
        out_shape=jax.ShapeDtypeStruct(x.shape, F32),
        compiler_params=pltpu.CompilerParams(
            dimension_semantics=("arbitrary", "arbitrary"), vmem_limit_bytes=VMEM_LIMIT_BYTES),
        name="dsw_out_proj",
    )(og, x, mod, w_out, final_g.reshape(1, D_MODEL))


def kernel(x, c, ada_w, ada_b, norm_g, gla_w_in, gla_w_alpha, gla_b_alpha, gla_norm_g, gla_w_out,
           dsw_w_in, dsw_w_out, rel_bias, final_g):
    batch = x.shape[0]
    mod = _adaln_mod(c, ada_w, ada_b)
    mod = mod.reshape(DEPTH, batch, 3, D_MODEL)
    x, *h_orders = _gla_layer(x, mod[0], norm_g[0], gla_w_in[0], gla_w_alpha[0], gla_b_alpha[0],
                              gla_norm_g[0], gla_w_out[0], mod[1], norm_g[1])
    projs = []
    for gi in range(N_GROUPS):
        sections = [3 * gi, 3 * gi + 1, 3 * gi + 2]
        if gi == 0:
            sections.append(3 * N_GROUPS)
        projs.append(_dsw_group_proj(h_orders[gi].reshape(-1, D_MODEL), dsw_w_in, sections))
    bias = _dsw_bias(rel_bias)
    og = _dsw_attention(projs, bias, batch)
    return _dsw_out_proj(og, x, mod[1], dsw_w_out[0], final_g)
```

```python
import functools
import math

import jax
import jax.numpy as jnp
import numpy as np
from jax import lax
from jax.experimental import pallas as pl
from jax.experimental.pallas import tpu as pltpu

F32 = jnp.float32
BF16 = jnp.bfloat16

D_MODEL = 1024
DEPTH = 2
NORM_EPS = 1e-6

GLA_HEADS = 4
GLA_DK = D_MODEL // 2
GLA_DV = D_MODEL
GLA_DK_HEAD = GLA_DK // GLA_HEADS
GLA_DV_HEAD = GLA_DV // GLA_HEADS
GLA_RANK = 16
GLA_TAU = 16.0
GLA_CHUNK = 64

DSW_HEADS = 16
DSW_HEAD_DIM = D_MODEL // DSW_HEADS
DSW_WIDTH = DSW_HEADS * DSW_HEAD_DIM
DSW_GROUPS = ((128, 1), (512, 4), (2048, 16))
N_GROUPS = len(DSW_GROUPS)
DSW_SPAN = 128
REGATHER_STRIDE = 4
assert N_GROUPS == 3 and all(b[1] == a[1] * REGATHER_STRIDE for a, b in zip(DSW_GROUPS, DSW_GROUPS[1:]))


def _segment_residues():
    table = [[0]]
    for gi in range(1, N_GROUPS):
        prev_dil = DSW_GROUPS[gi - 1][1]
        table.append([res + prev_dil * r2 for res in table[-1] for r2 in range(REGATHER_STRIDE)])
    return table


SEG_RESIDUE = _segment_residues()
REL_BUCKETS = 32
REL_MAX_DIST = 2048

LANES = 128
HEADS_PER_STEP = LANES // DSW_HEAD_DIM
assert HEADS_PER_STEP == 2
ATT_BATCH_PER_STEP = 2
LOG2E = math.log2(math.e)
QUERY_SCALE = DSW_HEAD_DIM ** -0.5 * LOG2E
VMEM_LIMIT_BYTES = 56 * 1024 * 1024

GLA_ROW_TILE = 512
BF16_SUBLANES = 16
assert GLA_ROW_TILE % GLA_CHUNK == 0 and (GLA_ROW_TILE // DSW_GROUPS[-1][1]) % BF16_SUBLANES == 0
OUT_ROW_TILE = 2048
PROJ_ROW_CHUNK = 512
PROJ_ROW_TILE = 4096

_NT = (((1,), (1,)), ((), ()))


def _dot(a, b, precision=None):
    return jnp.dot(a, b, preferred_element_type=F32, precision=precision)


def _dot_nt(a, b):
    return lax.dot_general(a, b, _NT, preferred_element_type=F32)


def _silu(v):
    return v * (1.0 / (1.0 + jnp.exp(-v)))


def _log_sigmoid(z):
    return jnp.minimum(z, 0.0) - jnp.log(1.0 + jnp.exp(-jnp.abs(z)))


def _chunk_cumsum(x, chunk):
    pos = lax.broadcasted_iota(jnp.int32, x.shape, 0) % chunk
    step = 1
    while step < chunk:
        x = x + jnp.where(pos >= step, pltpu.roll(x, shift=step, axis=0), 0.0)
        step *= 2
    return x


def _modulated_rmsnorm(x, g, shift, scale):
    y = x * lax.rsqrt(jnp.mean(x * x, axis=-1, keepdims=True) + NORM_EPS)
    return y * (g * (1.0 + scale)) + shift


def _adaln_kernel(c_ref, w_ref, b_ref, out_ref):
    c_act = _silu(c_ref[...])
    out_ref[...] = _dot(c_act.astype(BF16), w_ref[...].astype(BF16)) + b_ref[...]


def _adaln_mod(c, ada_w, ada_b):
    batch = c.shape[0]
    n_col = 3
    return pl.pallas_call(
        _adaln_kernel,
        grid=(DEPTH, n_col),
        in_specs=[
            pl.BlockSpec((batch, D_MODEL), lambda i, j: (0, 0)),
            pl.BlockSpec((None, D_MODEL, D_MODEL), lambda i, j: (i, 0, j)),
            pl.BlockSpec((None, 1, D_MODEL), lambda i, j: (i, 0, j)),
        ],
        out_specs=pl.BlockSpec((None, batch, D_MODEL), lambda i, j: (i, 0, j)),
        out_shape=jax.ShapeDtypeStruct((DEPTH, batch, 3 * D_MODEL), F32),
        compiler_params=pltpu.CompilerParams(
            dimension_semantics=("arbitrary", "arbitrary"), vmem_limit_bytes=VMEM_LIMIT_BYTES),
        name="adaln_mod",
    )(c, ada_w, ada_b.reshape(DEPTH, 1, 3 * D_MODEL))


def _gla_kernel(x_ref, mod_ref, ng_ref, wq_ref, wk_ref, wv_ref, wr_ref, wg_ref, wa_ref, ba_ref,
                hng_ref, wo_ref, next_mod_ref, next_ng_ref, out_ref, *rest):
    h_refs = rest[:N_GROUPS]
    state_ref, q_scr, k_scr, v_scr, b_scr, o_scr, hb_scr, hf_scr, tmp_scr = rest[N_GROUPS:]
    ts = GLA_ROW_TILE
    ch = GLA_CHUNK

    @pl.when(pl.program_id(1) == 0)
    def _():
        state_ref[...] = jnp.zeros_like(state_ref)

    shift, scale, gate = mod_ref[0:1, :], mod_ref[1:2, :], mod_ref[2:3, :]
    hb_scr[...] = _modulated_rmsnorm(x_ref[...], ng_ref[...], shift, scale).astype(BF16)
    g_lr = _dot(hb_scr[...], wg_ref[...])
    z = _dot(g_lr.astype(BF16), wa_ref[...]) + ba_ref[...]
    b_scr[...] = _chunk_cumsum(_log_sigmoid(z) / GLA_TAU, ch)
    q_scr[...] = _dot(hb_scr[...], wq_ref[...]) * (GLA_DK_HEAD ** -0.5)
    k_scr[...] = _dot(hb_scr[...], wk_ref[...])
    v_scr[...] = _dot(hb_scr[...], wv_ref[...])

    ci = lax.broadcasted_iota(jnp.int32, (ch, ch), 0)
    cj = lax.broadcasted_iota(jnp.int32, (ch, ch), 1)
    causal = cj <= ci
    n_chunks = ts // ch
    heads = range(GLA_HEADS)

    def k_lanes(hd):
        return pl.ds(hd * GLA_DK_HEAD, GLA_DK_HEAD)

    def v_lanes(hd):
        return pl.ds(hd * GLA_DV_HEAD, GLA_DV_HEAD)

    def intra_scores(c):
        rows = pl.ds(c * ch, ch)
        out = []
        for hd in heads:
            bc = b_scr[rows, k_lanes(hd)]
            mid = bc[ch // 2:ch // 2 + 1, :]
            qe = (q_scr[rows, k_lanes(hd)] * jnp.exp(bc - mid)).astype(BF16)
            ke = (k_scr[rows, k_lanes(hd)] * jnp.exp(mid - bc)).astype(BF16)
            out.append(_dot_nt(qe, ke))
        return out

    a_next = intra_scores(0)
    for c in range(n_chunks):
        rows = pl.ds(c * ch, ch)
        a_cur = a_next
        if c + 1 < n_chunks:
            a_next = intra_scores(c + 1)
        for hd in heads:
            bc = b_scr[rows, k_lanes(hd)]
            last = bc[ch - 1:ch, :]
            qc = q_scr[rows, k_lanes(hd)]
            kc = k_scr[rows, k_lanes(hd)]
            vcb = v_scr[rows, v_lanes(hd)].astype(BF16)
            st = state_ref[hd]
            kdt = (kc * jnp.exp(last - bc)).T.astype(BF16)
            a = jnp.where(causal, a_cur[hd], 0.0).astype(BF16)
            res = _dot(jnp.concatenate([a, kdt], axis=0), vcb)
            inter = _dot((qc * jnp.exp(bc)).astype(BF16), st.astype(BF16))
            decay = jnp.broadcast_to(jnp.exp(last), (GLA_DK_HEAD, GLA_DK_HEAD)).T
            decay = jnp.concatenate([decay] * (GLA_DV_HEAD // GLA_DK_HEAD), axis=1)
            state_ref[hd] = st * decay + res[ch:]
            o_scr[rows, v_lanes(hd)] = res[:ch] + inter

    r = _dot(hb_scr[...], wr_ref[...])
    parts = []
    for hd in range(GLA_HEADS):
        oh = o_scr[:, pl.ds(hd * GLA_DV_HEAD, GLA_DV_HEAD)]
        yh = oh * lax.rsqrt(jnp.mean(oh * oh, axis=-1, keepdims=True) + NORM_EPS)
        parts.append(yh * hng_ref[...])
    og = (jnp.concatenate(parts, axis=-1) * _silu(r)).astype(BF16)
    x_new = x_ref[...] + gate * _dot(og, wo_ref[...])
    out_ref[...] = x_new
    h_next = _modulated_rmsnorm(x_new, next_ng_ref[...], next_mod_ref[0:1, :], next_mod_ref[1:2, :])
    _store_group_orders(h_next, h_refs, hf_scr, tmp_scr)


def _gla_layer(x, mod, norm_g, w_in, w_alpha, b_alpha, head_norm_g, w_out, next_mod, next_norm_g):
    batch, seq, _ = x.shape
    ts = GLA_ROW_TILE
    o_q, o_k, o_v, o_g, o_r = 0, GLA_DK, 2 * GLA_DK, 2 * GLA_DK + GLA_DV, 2 * GLA_DK + GLA_DV + GLA_RANK
    w_in_b = w_in.astype(BF16)
    wq, wk, wv = w_in_b[:, o_q:o_k], w_in_b[:, o_k:o_v], w_in_b[:, o_v:o_g]
    wr = w_in_b[:, o_r:]
    wg = jnp.pad(w_in_b[:, o_g:o_r], ((0, 0), (0, LANES - GLA_RANK)))
    wa = jnp.pad(w_alpha, ((0, LANES - GLA_RANK), (0, 0))).astype(BF16)

    def const(shape):
        return pl.BlockSpec(shape, lambda b, t: (0,) * len(shape))

    return pl.pallas_call(
        _gla_kernel,
        grid=(batch, seq // ts),
        in_specs=[
            pl.BlockSpec((None, ts, D_MODEL), lambda b, t: (b, t, 0)),
            pl.BlockSpec((None, 3, D_MODEL), lambda b, t: (b, 0, 0)),
            const((1, D_MODEL)),
            const((D_MODEL, GLA_DK)), const((D_MODEL, GLA_DK)), const((D_MODEL, GLA_DV)),
            const((D_MODEL, GLA_DV)), const((D_MODEL, LANES)),
            const((LANES, GLA_DK)), const((1, GLA_DK)),
            const((1, GLA_DV_HEAD)), const((GLA_DV, D_MODEL)),
            pl.BlockSpec((None, 3, D_MODEL), lambda b, t: (b, 0, 0)),
            const((1, D_MODEL)),
        ],
        out_specs=[pl.BlockSpec((None, ts, D_MODEL), lambda b, t: (b, t, 0))] + [
            pl.BlockSpec((None, dil, ts // dil, D_MODEL), lambda b, t: (b, 0, t, 0)) for _, dil in DSW_GROUPS],
        out_shape=[jax.ShapeDtypeStruct(x.shape, F32)] + [
            jax.ShapeDtypeStruct((batch, dil, seq // dil, D_MODEL), BF16) for _, dil in DSW_GROUPS],
        scratch_shapes=[
            pltpu.VMEM((GLA_HEADS, GLA_DK_HEAD, GLA_DV_HEAD), F32),
            pltpu.VMEM((ts, GLA_DK), F32), pltpu.VMEM((ts, GLA_DK), F32),
            pltpu.VMEM((ts, GLA_DV), F32), pltpu.VMEM((ts, GLA_DK), F32),
            pltpu.VMEM((ts, GLA_DV), F32),
            pltpu.VMEM((ts, D_MODEL), BF16),
            pltpu.VMEM((D_MODEL // LANES, ts, LANES), F32),
            pltpu.VMEM((ts, LANES), F32),
        ],
        compiler_params=pltpu.CompilerParams(
            dimension_semantics=("arbitrary", "arbitrary"), vmem_limit_bytes=VMEM_LIMIT_BYTES),
        name="gla_layer",
    )(x, mod, norm_g.reshape(1, D_MODEL), wq, wk, wv, wr, wg, wa, b_alpha.reshape(1, GLA_DK),
      head_norm_g.reshape(1, GLA_DV_HEAD), w_out.astype(BF16), next_mod, next_norm_g.reshape(1, D_MODEL))


def _store_group_orders(h, out_refs, hf_scr, tmp_scr):
    n_rows = h.shape[0]
    out_refs[0][0] = h.astype(BF16)
    for cb in range(D_MODEL // LANES):
        hf_scr[cb] = h[:, cb * LANES:(cb + 1) * LANES]
    for cb in range(D_MODEL // LANES):
        lanes = pl.ds(cb * LANES, LANES)
        src = hf_scr.at[cb]
        for gi in range(1, N_GROUPS):
            prev_dil = DSW_GROUPS[gi - 1][1]
            prev_len = n_rows // prev_dil
            seg_len = prev_len // REGATHER_STRIDE
            for p in range(prev_dil):
                for r2 in range(REGATHER_STRIDE):
                    seg = p * REGATHER_STRIDE + r2
                    piece = src[pl.ds(p * prev_len + r2, seg_len, stride=REGATHER_STRIDE), :]
                    out_refs[gi][seg, :, lanes] = piece.astype(BF16)
                    if gi + 1 < N_GROUPS:
                        tmp_scr[pl.ds(seg * seg_len, seg_len), :] = piece
            src = tmp_scr


def _dsw_proj_kernel(h_ref, w_ref, out_ref):
    out_scale = jnp.where(pl.program_id(1) == 0, QUERY_SCALE, 1.0).astype(F32)
    w = w_ref[...].astype(BF16)
    for s in range(h_ref.shape[0] // PROJ_ROW_CHUNK):
        rows = pl.ds(s * PROJ_ROW_CHUNK, PROJ_ROW_CHUNK)
        out_ref[rows, :] = (_dot(h_ref[rows, :], w) * out_scale).astype(BF16)


def _dsw_group_proj(h, w_in, sections):
    m_rows = h.shape[0]
    n_sec = len(sections)
    tm = PROJ_ROW_TILE

    def w_block(i, j):
        col = functools.reduce(lambda a, b: a + b, [jnp.where(j == k, sec, 0) for k, sec in enumerate(sections)])
        return (0, 0, col)

    return pl.pallas_call(
        _dsw_proj_kernel,
        grid=(m_rows // tm, n_sec),
        in_specs=[
            pl.BlockSpec((tm, D_MODEL), lambda i, j: (i, 0)),
            pl.BlockSpec((None, D_MODEL, DSW_WIDTH), w_block),
        ],
        out_specs=pl.BlockSpec((None, tm, DSW_WIDTH), lambda i, j: (j, i, 0)),
        out_shape=jax.ShapeDtypeStruct((n_sec, m_rows, DSW_WIDTH), BF16),
        compiler_params=pltpu.CompilerParams(
            dimension_semantics=("arbitrary", "arbitrary"), vmem_limit_bytes=VMEM_LIMIT_BYTES),
        name="dsw_group_proj",
    )(h, w_in)


def _t5_causal_bucket(n):
    max_exact = REL_BUCKETS // 2
    nf = np.maximum(n, 1).astype(np.float32)
    large = max_exact + (np.log(nf / max_exact) / math.log(REL_MAX_DIST / max_exact)
                         * (REL_BUCKETS - max_exact)).astype(np.int32)
    large = np.minimum(large, REL_BUCKETS - 1)
    return np.where(n < max_exact, n, large).astype(np.int32)


def _bucket_tables():
    qi = np.arange(DSW_SPAN)[:, None]
    kj = np.arange(2 * DSW_SPAN)[None, :]
    steps = qi + DSW_SPAN - kj
    in_window = (steps >= 0) & (steps <= DSW_SPAN)
    tables = []
    for _, dil in DSW_GROUPS:
        bucket = _t5_causal_bucket(np.clip(steps, 0, DSW_SPAN) * dil)
        tables.append(np.where(in_window, bucket, -1))
    return np.stack(tables).astype(np.int32)


def _dsw_bias_kernel(rel_ref, bucket_ref, out_ref):
    bucket = bucket_ref[...]
    masks = [bucket == bk for bk in range(REL_BUCKETS)]
    outside = jnp.where(bucket < 0, -jnp.inf, 0.0).astype(F32)
    for h in range(DSW_HEADS):
        row = outside
        for bk in range(REL_BUCKETS):
            row = jnp.where(masks[bk], rel_ref[bk, h] * LOG2E, row)
        tile = jnp.broadcast_to(row, (DSW_SPAN, 2 * DSW_SPAN))
        out_ref[pl.ds(h * DSW_SPAN, DSW_SPAN), :] = pltpu.roll(tile, shift=0, axis=1, stride=1, stride_axis=0)


def _dsw_bias(rel_bias):
    return pl.pallas_call(
        _dsw_bias_kernel,
        grid=(N_GROUPS,),
        in_specs=[
            pl.BlockSpec(memory_space=pltpu.SMEM),
            pl.BlockSpec((None, 1, 2 * DSW_SPAN), lambda g: (g, 0, 0)),
        ],
        out_specs=pl.BlockSpec((None, DSW_HEADS * DSW_SPAN, 2 * DSW_SPAN), lambda g: (g, 0, 0)),
        out_shape=jax.ShapeDtypeStruct((N_GROUPS, DSW_HEADS * DSW_SPAN, 2 * DSW_SPAN), F32),
        compiler_params=pltpu.CompilerParams(dimension_semantics=("arbitrary",)),
        name="dsw_bias",
    )(rel_bias, jnp.asarray(_bucket_tables()[:, :1, :]))


def _dsw_attention_kernel(*refs):
    qkv_refs = refs[:3 * N_GROUPS]
    gate_ref, bias_ref, out_ref, pv_scr, m_scr, l_scr = refs[3 * N_GROUPS:]
    n_batch, seq, _ = out_ref.shape
    span = DSW_SPAN
    lane = lax.broadcasted_iota(jnp.int32, (span, LANES), 1)
    head_masks = [(lane // DSW_HEAD_DIM) == hh for hh in range(HEADS_PER_STEP)]

    def is_first(gi, blk):
        return blk % (seq // DSW_GROUPS[gi][1] // span) == 0

    def key_rows(gi, blk):
        return pl.ds(blk * span, span) if is_first(gi, blk) else pl.ds((blk - 1) * span, 2 * span)

    def attend(bb):
        def scores(gi, blk):
            q = qkv_refs[3 * gi][bb, pl.ds(blk * span, span), :]
            q2 = jnp.concatenate([jnp.where(mk, q, jnp.zeros_like(q)) for mk in head_masks], axis=0)
            return _dot_nt(q2, qkv_refs[3 * gi + 1][bb, key_rows(gi, blk), :])

        def biased_max(gi, blk, s):
            if is_first(gi, blk):
                s = s + bias_ref[gi, :, pl.ds(span, span)]
            else:
                s = s + bias_ref[gi]
            return s, jnp.max(s, axis=-1, keepdims=True)

        units = [(gi, blk) for gi in range(N_GROUPS) for blk in range(seq // span)]
        raw = {i: scores(*units[i]) for i in range(2)}
        staged = {0: biased_max(*units[0], raw.pop(0))}
        for idx, (gi, blk) in enumerate(units):
            dil = DSW_GROUPS[gi][1]
            if idx + 2 < len(units):
                raw[idx + 2] = scores(*units[idx + 2])
            if idx + 1 < len(units):
                staged[idx + 1] = biased_max(*units[idx + 1], raw.pop(idx + 1))
            s, m = staged.pop(idx)
            e = jnp.exp2(s - m)
            l = jnp.sum(e, axis=-1, keepdims=True)
            pv = _dot(e.astype(BF16), qkv_refs[3 * gi + 2][bb, key_rows(gi, blk), :])
            blocks_per_seg = seq // dil // span
            residue = SEG_RESIDUE[gi][blk // blocks_per_seg]
            i0 = (blk % blocks_per_seg) * span
            dst = pl.ds(i0 * dil + residue, span, stride=dil) if dil > 1 else pl.ds(blk * span, span)
            pv_scr[gi, dst, :] = jnp.where(head_masks[0], pv[:span], pv[span:])
            m_scr[gi, dst, :] = jnp.where(head_masks[0], m[:span], m[span:])
            l_scr[gi, dst, :] = jnp.where(head_masks[0], l[:span], l[span:])

        tops = [m_scr[g] for g in range(N_GROUPS)]
        top = functools.reduce(jnp.maximum, tops)
        wts = [jnp.exp2(v - top) for v in tops]
        num = functools.reduce(lambda a, b: a + b, [wts[g] * pv_scr[g] for g in range(N_GROUPS)])
        den = functools.reduce(lambda a, b: a + b, [wts[g] * l_scr[g] for g in range(N_GROUPS)])
        out_ref[bb] = (num * _silu(gate_ref[bb].astype(F32)) / den).astype(BF16)

    def body(bb, carry):
        attend(bb)
        return carry

    lax.fori_loop(0, n_batch, body, 0)


def _dsw_attention(projs, bias, batch):
    seq = projs[0].shape[1] // batch
    n_pairs = DSW_HEADS // HEADS_PER_STEP
    bps = ATT_BATCH_PER_STEP
    operands, in_specs = [], []

    def add(arr, j):
        operands.append(arr.reshape(arr.shape[0], batch, seq, DSW_WIDTH))
        in_specs.append(pl.BlockSpec((None, bps, seq, LANES), lambda b, hp, j=j: (j, b, 0, hp)))

    for gi in range(N_GROUPS):
        for j in range(3):
            add(projs[gi], j)
    add(projs[0], 3)
    in_specs.append(
        pl.BlockSpec((N_GROUPS, HEADS_PER_STEP * DSW_SPAN, 2 * DSW_SPAN), lambda b, hp: (0, hp, 0)))
    return pl.pallas_call(
        _dsw_attention_kernel,
        grid=(batch // bps, n_pairs),
        in_specs=in_specs,
        out_specs=pl.BlockSpec((bps, seq, LANES), lambda b, hp: (b, 0, hp)),
        out_shape=jax.ShapeDtypeStruct((batch, seq, DSW_WIDTH), BF16),
        scratch_shapes=[pltpu.VMEM((N_GROUPS, seq, LANES), F32)] * 3,
        compiler_params=pltpu.CompilerParams(
            dimension_semantics=("arbitrary", "arbitrary"), vmem_limit_bytes=VMEM_LIMIT_BYTES),
        name="dsw_attention",
    )(*operands, bias)


def _dsw_out_proj_kernel(o_ref, x_ref, mod_ref, w_ref, fg_ref, out_ref):
    gate = mod_ref[2:3, :]
    w = w_ref[...].astype(BF16)
    for s in range(o_ref.shape[0] // PROJ_ROW_CHUNK):
        rows = pl.ds(s * PROJ_ROW_CHUNK, PROJ_ROW_CHUNK)
        xn = x_ref[rows, :] + gate * _dot(o_ref[rows, :], w)
        out_ref[rows, :] = xn * lax.rsqrt(jnp.mean(xn * xn, axis=-1, keepdims=True) + NORM_EPS) * fg_ref[...]


def _dsw_out_proj(og, x, mod, w_out, final_g):
    batch, seq, _ = x.shape
    ts = OUT_ROW_TILE
    return pl.pallas_call(
        _dsw_out_proj_kernel,
        grid=(batch, seq // ts),
        in_specs=[
            pl.BlockSpec((None, ts, DSW_WIDTH), lambda b, t: (b, t, 0)),
            pl.BlockSpec((None, ts, D_MODEL), lambda b, t: (b, t, 0)),
            pl.BlockSpec((None, 3, D_MODEL), lambda b, t: (b, 0, 0)),
            pl.BlockSpec((DSW_WIDTH, D_MODEL), lambda b, t: (0, 0)),
            pl.BlockSpec((1, D_MODEL), lambda b, t: (0, 0)),
        ],
        out_specs=pl.BlockSpec((None, ts, D_MODEL), lambda b, t: (b, t, 0)),
        out_shape=jax.ShapeDtypeStruct(x.shape, F32),
        compiler_params=pltpu.CompilerParams(
            dimension_semantics=("arbitrary", "arbitrary"), vmem_limit_bytes=VMEM_LIMIT_BYTES),
        name="dsw_out_proj",
    )(og, x, mod, w_out, final_g.reshape(1, D_MODEL))


def kernel(x, c, ada_w, ada_b, norm_g, gla_w_in, gla_w_alpha, gla_b_alpha, gla_norm_g, gla_w_out,
           dsw_w_in, dsw_w_out, rel_bias, final_g):
    batch, seq, d_model = x.shape
    assert d_model == D_MODEL and x.dtype == F32 and ada_w.shape[0] == DEPTH
    assert seq % GLA_ROW_TILE == 0 and seq % OUT_ROW_TILE == 0 and seq % (DSW_SPAN * DSW_GROUPS[-1][1]) == 0
    assert (batch * seq) % PROJ_ROW_TILE == 0 and batch % ATT_BATCH_PER_STEP == 0
    mod = _adaln_mod(c, ada_w, ada_b)
    mod = mod.reshape(DEPTH, batch, 3, D_MODEL)
    x, *h_orders = _gla_layer(x, mod[0], norm_g[0], gla_w_in[0], gla_w_alpha[0], gla_b_alpha[0],
                              gla_norm_g[0], gla_w_out[0], mod[1], norm_g[1])
    projs = []
    for gi in range(N_GROUPS):
        sections = [3 * gi, 3 * gi + 1, 3 * gi + 2]
        if gi == 0:
            sections.append(3 * N_GROUPS)
        projs.append(_dsw_group_proj(h_orders[gi].reshape(-1, D_MODEL), dsw_w_in, sections))
    bias = _dsw_bias(rel_bias)
    og = _dsw_attention(projs, bias, batch)
    return _dsw_out_proj(og, x, mod[1], dsw_w_out[0], final_g)
```

```python
import functools
import math

import jax
import jax.numpy as jnp
import numpy as np
from jax import lax
from jax.experimental import pallas as pl
from jax.experimental.pallas import tpu as pltpu

F32 = jnp.float32
BF16 = jnp.bfloat16

D_MODEL = 1024
DEPTH = 2
NORM_EPS = 1e-6

GLA_HEADS = 4
GLA_DK = D_MODEL // 2
GLA_DV = D_MODEL
GLA_DK_HEAD = GLA_DK // GLA_HEADS
GLA_DV_HEAD = GLA_DV // GLA_HEADS
GLA_RANK = 16
GLA_TAU = 16.0
GLA_CHUNK = 64

DSW_HEADS = 16
DSW_HEAD_DIM = D_MODEL // DSW_HEADS
DSW_WIDTH = DSW_HEADS * DSW_HEAD_DIM
DSW_GROUPS = ((128, 1), (512, 4), (2048, 16))
N_GROUPS = len(DSW_GROUPS)
DSW_SPAN = 128
REGATHER_STRIDE = 4
assert N_GROUPS == 3 and all(b[1] == a[1] * REGATHER_STRIDE for a, b in zip(DSW_GROUPS, DSW_GROUPS[1:]))


def _segment_residues():
    table = [[0]]
    for gi in range(1, N_GROUPS):
        prev_dil = DSW_GROUPS[gi - 1][1]
        table.append([res + prev_dil * r2 for res in table[-1] for r2 in range(REGATHER_STRIDE)])
    return table


SEG_RESIDUE = _segment_residues()
REL_BUCKETS = 32
REL_MAX_DIST = 2048

LANES = 128
HEADS_PER_STEP = LANES // DSW_HEAD_DIM
assert HEADS_PER_STEP == 2
ATT_BATCH_PER_STEP = 2
LOG2E = math.log2(math.e)
QUERY_SCALE = DSW_HEAD_DIM ** -0.5 * LOG2E
VMEM_LIMIT_BYTES = 56 * 1024 * 1024

GLA_ROW_TILE = 512
BF16_SUBLANES = 16
assert GLA_ROW_TILE % GLA_CHUNK == 0 and (GLA_ROW_TILE // DSW_GROUPS[-1][1]) % BF16_SUBLANES == 0
OUT_ROW_TILE = 2048
PROJ_ROW_CHUNK = 512
PROJ_ROW_TILE = 4096

_NT = (((1,), (1,)), ((), ()))


def _dot(a, b, precision=None):
    return jnp.dot(a, b, preferred_element_type=F32, precision=precision)


def _dot_nt(a, b):
    return lax.dot_general(a, b, _NT, preferred_element_type=F32)


def _silu(v):
    return v * (1.0 / (1.0 + jnp.exp(-v)))


def _log_sigmoid(z):
    return jnp.minimum(z, 0.0) - jnp.log(1.0 + jnp.exp(-jnp.abs(z)))


def _chunk_cumsum(x, chunk):
    pos = lax.broadcasted_iota(jnp.int32, x.shape, 0) % chunk
    step = 1
    while step < chunk:
        x = x + jnp.where(pos >= step, pltpu.roll(x, shift=step, axis=0), 0.0)
        step *= 2
    return x


def _modulated_rmsnorm(x, g, shift, scale):
    y = x * lax.rsqrt(jnp.mean(x * x, axis=-1, keepdims=True) + NORM_EPS)
    return y * (g * (1.0 + scale)) + shift


def _adaln_kernel(c_ref, w_ref, b_ref, out_ref):
    c_act = _silu(c_ref[...])
    out_ref[...] = _dot(c_act.astype(BF16), w_ref[...].astype(BF16)) + b_ref[...]


def _adaln_mod(c, ada_w, ada_b):
    batch = c.shape[0]
    n_col = 3
    return pl.pallas_call(
        _adaln_kernel,
        grid=(DEPTH, n_col),
        in_specs=[
            pl.BlockSpec((batch, D_MODEL), lambda i, j: (0, 0)),
            pl.BlockSpec((None, D_MODEL, D_MODEL), lambda i, j: (i, 0, j)),
            pl.BlockSpec((None, 1, D_MODEL), lambda i, j: (i, 0, j)),
        ],
        out_specs=pl.BlockSpec((None, batch, D_MODEL), lambda i, j: (i, 0, j)),
        out_shape=jax.ShapeDtypeStruct((DEPTH, batch, 3 * D_MODEL), F32),
        compiler_params=pltpu.CompilerParams(
            dimension_semantics=("arbitrary", "arbitrary"), vmem_limit_bytes=VMEM_LIMIT_BYTES),
        name="adaln_mod",
    )(c, ada_w, ada_b.reshape(DEPTH, 1, 3 * D_MODEL))


def _gla_kernel(x_ref, mod_ref, ng_ref, wq_ref, wk_ref, wv_ref, wr_ref, wg_ref, wa_ref, ba_ref,
                hng_ref, wo_ref, next_mod_ref, next_ng_ref, out_ref, *rest):
    h_refs = rest[:N_GROUPS]
    state_ref, q_scr, k_scr, v_scr, b_scr, o_scr, hb_scr, hf_scr, tmp_scr = rest[N_GROUPS:]
    ts = GLA_ROW_TILE
    ch = GLA_CHUNK

    @pl.when(pl.program_id(1) == 0)
    def _():
        state_ref[...] = jnp.zeros_like(state_ref)

    shift, scale, gate = mod_ref[0:1, :], mod_ref[1:2, :], mod_ref[2:3, :]
    hb_scr[...] = _modulated_rmsnorm(x_ref[...], ng_ref[...], shift, scale).astype(BF16)
    g_lr = _dot(hb_scr[...], wg_ref[...])
    z = _dot(g_lr.astype(BF16), wa_ref[...]) + ba_ref[...]
    b_scr[...] = _chunk_cumsum(_log_sigmoid(z) / GLA_TAU, ch)
    q_scr[...] = _dot(hb_scr[...], wq_ref[...]) * (GLA_DK_HEAD ** -0.5)
    k_scr[...] = _dot(hb_scr[...], wk_ref[...])
    v_scr[...] = _dot(hb_scr[...], wv_ref[...])

    ci = lax.broadcasted_iota(jnp.int32, (ch, ch), 0)
    cj = lax.broadcasted_iota(jnp.int32, (ch, ch), 1)
    causal = cj <= ci
    n_chunks = ts // ch
    heads = range(GLA_HEADS)

    def k_lanes(hd):
        return pl.ds(hd * GLA_DK_HEAD, GLA_DK_HEAD)

    def v_lanes(hd):
        return pl.ds(hd * GLA_DV_HEAD, GLA_DV_HEAD)

    def intra_scores(c):
        rows = pl.ds(c * ch, ch)
        out = []
        for hd in heads:
            bc = b_scr[rows, k_lanes(hd)]
            mid = bc[ch // 2:ch // 2 + 1, :]
            qe = (q_scr[rows, k_lanes(hd)] * jnp.exp(bc - mid)).astype(BF16)
            ke = (k_scr[rows, k_lanes(hd)] * jnp.exp(mid - bc)).astype(BF16)
            out.append(_dot_nt(qe, ke))
        return out

    a_next = intra_scores(0)
    for c in range(n_chunks):
        rows = pl.ds(c * ch, ch)
        a_cur = a_next
        if c + 1 < n_chunks:
            a_next = intra_scores(c + 1)
        for hd in heads:
            bc = b_scr[rows, k_lanes(hd)]
            last = bc[ch - 1:ch, :]
            qc = q_scr[rows, k_lanes(hd)]
            kc = k_scr[rows, k_lanes(hd)]
            vcb = v_scr[rows, v_lanes(hd)].astype(BF16)
            st = state_ref[hd]
            kdt = (kc * jnp.exp(last - bc)).T.astype(BF16)
            a = jnp.where(causal, a_cur[hd], 0.0).astype(BF16)
            res = _dot(jnp.concatenate([a, kdt], axis=0), vcb)
            inter = _dot((qc * jnp.exp(bc)).astype(BF16), st.astype(BF16))
            decay = jnp.broadcast_to(jnp.exp(last), (GLA_DK_HEAD, GLA_DK_HEAD)).T
            decay = jnp.concatenate([decay] * (GLA_DV_HEAD // GLA_DK_HEAD), axis=1)
            state_ref[hd] = st * decay + res[ch:]
            o_scr[rows, v_lanes(hd)] = res[:ch] + inter

    r = _dot(hb_scr[...], wr_ref[...])
    parts = []
    for hd in range(GLA_HEADS):
        oh = o_scr[:, pl.ds(hd * GLA_DV_HEAD, GLA_DV_HEAD)]
        yh = oh * lax.rsqrt(jnp.mean(oh * oh, axis=-1, keepdims=True) + NORM_EPS)
        parts.append(yh * hng_ref[...])
    og = (jnp.concatenate(parts, axis=-1) * _silu(r)).astype(BF16)
    x_new = x_ref[...] + gate * _dot(og, wo_ref[...])
    out_ref[...] = x_new
    h_next = _modulated_rmsnorm(x_new, next_ng_ref[...], next_mod_ref[0:1, :], next_mod_ref[1:2, :])
    _store_group_orders(h_next, h_refs, hf_scr, tmp_scr)


def _gla_split_kernel(w_ref, wq_ref, wk_ref, wv_ref, wr_ref, wg_ref):
    o_k, o_v, o_g, o_r = GLA_DK, 2 * GLA_DK, 2 * GLA_DK + GLA_DV, 2 * GLA_DK + GLA_DV + GLA_RANK
    wq_ref[...] = w_ref[:, 0:o_k].astype(BF16)
    wk_ref[...] = w_ref[:, o_k:o_v].astype(BF16)
    wv_ref[...] = w_ref[:, o_v:o_g].astype(BF16)
    wr_ref[...] = w_ref[:, o_r:o_r + GLA_DV].astype(BF16)
    wg_ref[...] = jnp.zeros_like(wg_ref)
    wg_ref[:, 0:GLA_RANK] = w_ref[:, o_g:o_r].astype(BF16)


def _gla_split_weights(w_in):
    shapes = [(D_MODEL, GLA_DK), (D_MODEL, GLA_DK), (D_MODEL, GLA_DV), (D_MODEL, GLA_DV), (D_MODEL, LANES)]
    return pl.pallas_call(
        _gla_split_kernel,
        out_shape=[jax.ShapeDtypeStruct(s, BF16) for s in shapes],
        compiler_params=pltpu.CompilerParams(vmem_limit_bytes=VMEM_LIMIT_BYTES),
        name="gla_split_weights",
    )(w_in)


def _gla_layer(x, mod, norm_g, w_in, w_alpha, b_alpha, head_norm_g, w_out, next_mod, next_norm_g):
    batch, seq, _ = x.shape
    ts = GLA_ROW_TILE
    wq, wk, wv, wr, wg = _gla_split_weights(w_in)
    wa = jnp.pad(w_alpha, ((0, LANES - GLA_RANK), (0, 0))).astype(BF16)

    def const(shape):
        return pl.BlockSpec(shape, lambda b, t: (0,) * len(shape))

    return pl.pallas_call(
        _gla_kernel,
        grid=(batch, seq // ts),
        in_specs=[
            pl.BlockSpec((None, ts, D_MODEL), lambda b, t: (b, t, 0)),
            pl.BlockSpec((None, 3, D_MODEL), lambda b, t: (b, 0, 0)),
            const((1, D_MODEL)),
            const((D_MODEL, GLA_DK)), const((D_MODEL, GLA_DK)), const((D_MODEL, GLA_DV)),
            const((D_MODEL, GLA_DV)), const((D_MODEL, LANES)),
            const((LANES, GLA_DK)), const((1, GLA_DK)),
            const((1, GLA_DV_HEAD)), const((GLA_DV, D_MODEL)),
            pl.BlockSpec((None, 3, D_MODEL), lambda b, t: (b, 0, 0)),
            const((1, D_MODEL)),
        ],
        out_specs=[pl.BlockSpec((None, ts, D_MODEL), lambda b, t: (b, t, 0))] + [
            pl.BlockSpec((None, dil, ts // dil, D_MODEL), lambda b, t: (b, 0, t, 0)) for _, dil in DSW_GROUPS],
        out_shape=[jax.ShapeDtypeStruct(x.shape, F32)] + [
            jax.ShapeDtypeStruct((batch, dil, seq // dil, D_MODEL), BF16) for _, dil in DSW_GROUPS],
        scratch_shapes=[
            pltpu.VMEM((GLA_HEADS, GLA_DK_HEAD, GLA_DV_HEAD), F32),
            pltpu.VMEM((ts, GLA_DK), F32), pltpu.VMEM((ts, GLA_DK), F32),
            pltpu.VMEM((ts, GLA_DV), F32), pltpu.VMEM((ts, GLA_DK), F32),
            pltpu.VMEM((ts, GLA_DV), F32),
            pltpu.VMEM((ts, D_MODEL), BF16),
            pltpu.VMEM((D_MODEL // LANES, ts, LANES), F32),
            pltpu.VMEM((ts, LANES), F32),
        ],
        compiler_params=pltpu.CompilerParams(
            dimension_semantics=("arbitrary", "arbitrary"), vmem_limit_bytes=VMEM_LIMIT_BYTES),
        name="gla_layer",
    )(x, mod, norm_g.reshape(1, D_MODEL), wq, wk, wv, wr, wg, wa, b_alpha.reshape(1, GLA_DK),
      head_norm_g.reshape(1, GLA_DV_HEAD), w_out.astype(BF16), next_mod, next_norm_g.reshape(1, D_MODEL))


def _store_group_orders(h, out_refs, hf_scr, tmp_scr):
    n_rows = h.shape[0]
    out_refs[0][0] = h.astype(BF16)
    for cb in range(D_MODEL // LANES):
        hf_scr[cb] = h[:, cb * LANES:(cb + 1) * LANES]
    for cb in range(D_MODEL // LANES):
        lanes = pl.ds(cb * LANES, LANES)
        src = hf_scr.at[cb]
        for gi in range(1, N_GROUPS):
            prev_dil = DSW_GROUPS[gi - 1][1]
            prev_len = n_rows // prev_dil
            seg_len = prev_len // REGATHER_STRIDE
            for p in range(prev_dil):
                for r2 in range(REGATHER_STRIDE):
                    seg = p * REGATHER_STRIDE + r2
                    piece = src[pl.ds(p * prev_len + r2, seg_len, stride=REGATHER_STRIDE), :]
                    out_refs[gi][seg, :, lanes] = piece.astype(BF16)
                    if gi + 1 < N_GROUPS:
                        tmp_scr[pl.ds(seg * seg_len, seg_len), :] = piece
            src = tmp_scr


def _dsw_proj_kernel(h_ref, w_ref, out_ref):
    out_scale = jnp.where(pl.program_id(1) == 0, QUERY_SCALE, 1.0).astype(F32)
    w = w_ref[...].astype(BF16)
    for s in range(h_ref.shape[0] // PROJ_ROW_CHUNK):
        rows = pl.ds(s * PROJ_ROW_CHUNK, PROJ_ROW_CHUNK)
        out_ref[rows, :] = (_dot(h_ref[rows, :], w) * out_scale).astype(BF16)


def _dsw_group_proj(h, w_in, sections):
    m_rows = h.shape[0]
    n_sec = len(sections)
    tm = PROJ_ROW_TILE

    def w_block(i, j):
        col = functools.reduce(lambda a, b: a + b, [jnp.where(j == k, sec, 0) for k, sec in enumerate(sections)])
        return (0, 0, col)

    return pl.pallas_call(
        _dsw_proj_kernel,
        grid=(m_rows // tm, n_sec),
        in_specs=[
            pl.BlockSpec((tm, D_MODEL), lambda i, j: (i, 0)),
            pl.BlockSpec((None, D_MODEL, DSW_WIDTH), w_block),
        ],
        out_specs=pl.BlockSpec((None, tm, DSW_WIDTH), lambda i, j: (j, i, 0)),
        out_shape=jax.ShapeDtypeStruct((n_sec, m_rows, DSW_WIDTH), BF16),
        compiler_params=pltpu.CompilerParams(
            dimension_semantics=("arbitrary", "arbitrary"), vmem_limit_bytes=VMEM_LIMIT_BYTES),
        name="dsw_group_proj",
    )(h, w_in)


def _t5_causal_bucket(n):
    max_exact = REL_BUCKETS // 2
    nf = np.maximum(n, 1).astype(np.float32)
    large = max_exact + (np.log(nf / max_exact) / math.log(REL_MAX_DIST / max_exact)
                         * (REL_BUCKETS - max_exact)).astype(np.int32)
    large = np.minimum(large, REL_BUCKETS - 1)
    return np.where(n < max_exact, n, large).astype(np.int32)


def _bucket_tables():
    qi = np.arange(DSW_SPAN)[:, None]
    kj = np.arange(2 * DSW_SPAN)[None, :]
    steps = qi + DSW_SPAN - kj
    in_window = (steps >= 0) & (steps <= DSW_SPAN)
    tables = []
    for _, dil in DSW_GROUPS:
        bucket = _t5_causal_bucket(np.clip(steps, 0, DSW_SPAN) * dil)
        tables.append(np.where(in_window, bucket, -1))
    return np.stack(tables).astype(np.int32)


def _dsw_bias_kernel(rel_ref, bucket_ref, out_ref):
    bucket = bucket_ref[...]
    masks = [bucket == bk for bk in range(REL_BUCKETS)]
    outside = jnp.where(bucket < 0, -jnp.inf, 0.0).astype(F32)
    for h in range(DSW_HEADS):
        row = outside
        for bk in range(REL_BUCKETS):
            row = jnp.where(masks[bk], rel_ref[bk, h] * LOG2E, row)
        tile = jnp.broadcast_to(row, (DSW_SPAN, 2 * DSW_SPAN))
        out_ref[pl.ds(h * DSW_SPAN, DSW_SPAN), :] = pltpu.roll(tile, shift=0, axis=1, stride=1, stride_axis=0)


def _dsw_bias(rel_bias):
    return pl.pallas_call(
        _dsw_bias_kernel,
        grid=(N_GROUPS,),
        in_specs=[
            pl.BlockSpec(memory_space=pltpu.SMEM),
            pl.BlockSpec((None, 1, 2 * DSW_SPAN), lambda g: (g, 0, 0)),
        ],
        out_specs=pl.BlockSpec((None, DSW_HEADS * DSW_SPAN, 2 * DSW_SPAN), lambda g: (g, 0, 0)),
        out_shape=jax.ShapeDtypeStruct((N_GROUPS, DSW_HEADS * DSW_SPAN, 2 * DSW_SPAN), F32),
        compiler_params=pltpu.CompilerParams(dimension_semantics=("arbitrary",)),
        name="dsw_bias",
    )(rel_bias, jnp.asarray(_bucket_tables()[:, :1, :]))


def _dsw_attention_kernel(*refs):
    qkv_refs = refs[:3 * N_GROUPS]
    gate_ref, bias_ref, out_ref, pv_scr, m_scr, l_scr = refs[3 * N_GROUPS:]
    n_batch, seq, _ = out_ref.shape
    span = DSW_SPAN
    lane = lax.broadcasted_iota(jnp.int32, (span, LANES), 1)
    head_masks = [(lane // DSW_HEAD_DIM) == hh for hh in range(HEADS_PER_STEP)]

    def is_first(gi, blk):
        return blk % (seq // DSW_GROUPS[gi][1] // span) == 0

    def key_rows(gi, blk):
        return pl.ds(blk * span, span) if is_first(gi, blk) else pl.ds((blk - 1) * span, 2 * span)

    def attend(bb):
        def scores(gi, blk):
            q = qkv_refs[3 * gi][bb, pl.ds(blk * span, span), :]
            q2 = jnp.concatenate([jnp.where(mk, q, jnp.zeros_like(q)) for mk in head_masks], axis=0)
            return _dot_nt(q2, qkv_refs[3 * gi + 1][bb, key_rows(gi, blk), :])

        def biased_max(gi, blk, s):
            if is_first(gi, blk):
                s = s + bias_ref[gi, :, pl.ds(span, span)]
            else:
                s = s + bias_ref[gi]
            return s, jnp.max(s, axis=-1, keepdims=True)

        units = [(gi, blk) for gi in range(N_GROUPS) for blk in range(seq // span)]
        raw = {i: scores(*units[i]) for i in range(2)}
        staged = {0: biased_max(*units[0], raw.pop(0))}
        for idx, (gi, blk) in enumerate(units):
            dil = DSW_GROUPS[gi][1]
            if idx + 2 < len(units):
                raw[idx + 2] = scores(*units[idx + 2])
            if idx + 1 < len(units):
                staged[idx + 1] = biased_max(*units[idx + 1], raw.pop(idx + 1))
            s, m = staged.pop(idx)
            e = jnp.exp2(s - m)
            l = jnp.sum(e, axis=-1, keepdims=True)
            pv = _dot(e.astype(BF16), qkv_refs[3 * gi + 2][bb, key_rows(gi, blk), :])
            blocks_per_seg = seq // dil // span
            residue = SEG_RESIDUE[gi][blk // blocks_per_seg]
            i0 = (blk % blocks_per_seg) * span
            dst = pl.ds(i0 * dil + residue, span, stride=dil) if dil > 1 else pl.ds(blk * span, span)
            pv_scr[gi, dst, :] = jnp.where(head_masks[0], pv[:span], pv[span:])
            m_scr[gi, dst, :] = jnp.where(head_masks[0], m[:span], m[span:])
            l_scr[gi, dst, :] = jnp.where(head_masks[0], l[:span], l[span:])

        tops = [m_scr[g] for g in range(N_GROUPS)]
        top = functools.reduce(jnp.maximum, tops)
        wts = [jnp.exp2(v - top) for v in tops]
        num = functools.reduce(lambda a, b: a + b, [wts[g] * pv_scr[g] for g in range(N_GROUPS)])
        den = functools.reduce(lambda a, b: a + b, [wts[g] * l_scr[g] for g in range(N_GROUPS)])
        out_ref[bb] = (num * _silu(gate_ref[bb].astype(F32)) / den).astype(BF16)

    def body(bb, carry):
        attend(bb)
        return carry

    lax.fori_loop(0, n_batch, body, 0)


def _dsw_attention(projs, bias, batch):
    seq = projs[0].shape[1] // batch
    n_pairs = DSW_HEADS // HEADS_PER_STEP
    bps = ATT_BATCH_PER_STEP
    operands, in_specs = [], []

    def add(arr, j):
        operands.append(arr.reshape(arr.shape[0], batch, seq, DSW_WIDTH))
        in_specs.append(pl.BlockSpec((None, bps, seq, LANES), lambda b, hp, j=j: (j, b, 0, hp)))

    for gi in range(N_GROUPS):
        for j in range(3):
            add(projs[gi], j)
    add(projs[0], 3)
    in_specs.append(
        pl.BlockSpec((N_GROUPS, HEADS_PER_STEP * DSW_SPAN, 2 * DSW_SPAN), lambda b, hp: (0, hp, 0)))
    return pl.pallas_call(
        _dsw_attention_kernel,
        grid=(batch // bps, n_pairs),
        in_specs=in_specs,
        out_specs=pl.BlockSpec((bps, seq, LANES), lambda b, hp: (b, 0, hp)),
        out_shape=jax.ShapeDtypeStruct((batch, seq, DSW_WIDTH), BF16),
        scratch_shapes=[pltpu.VMEM((N_GROUPS, seq, LANES), F32)] * 3,
        compiler_params=pltpu.CompilerParams(
            dimension_semantics=("arbitrary", "arbitrary"), vmem_limit_bytes=VMEM_LIMIT_BYTES),
        name="dsw_attention",
    )(*operands, bias)


def _dsw_out_proj_kernel(o_ref, x_ref, mod_ref, w_ref, fg_ref, out_ref):
    gate = mod_ref[2:3, :]
    w = w_ref[...].astype(BF16)
    for s in range(o_ref.shape[0] // PROJ_ROW_CHUNK):
        rows = pl.ds(s * PROJ_ROW_CHUNK, PROJ_ROW_CHUNK)
        xn = x_ref[rows, :] + gate * _dot(o_ref[rows, :], w)
        out_ref[rows, :] = xn * lax.rsqrt(jnp.mean(xn * xn, axis=-1, keepdims=True) + NORM_EPS) * fg_ref[...]


def _dsw_out_proj(og, x, mod, w_out, final_g):
    batch, seq, _ = x.shape
    ts = OUT_ROW_TILE
    return pl.pallas_call(
        _dsw_out_proj_kernel,
        grid=(batch, seq // ts),
        in_specs=[
            pl.BlockSpec((None, ts, DSW_WIDTH), lambda b, t: (b, t, 0)),
            pl.BlockSpec((None, ts, D_MODEL), lambda b, t: (b, t, 0)),
            pl.BlockSpec((None, 3, D_MODEL), lambda b, t: (b, 0, 0)),
            pl.BlockSpec((DSW_WIDTH, D_MODEL), lambda b, t: (0, 0)),
            pl.BlockSpec((1, D_MODEL), lambda b, t: (0, 0)),
        ],
        out_specs=pl.BlockSpec((None, ts, D_MODEL), lambda b, t: (b, t, 0)),
        out_shape=jax.ShapeDtypeStruct(x.shape, F32),
        compiler_params=pltpu.CompilerParams(
            dimension_semantics=("arbitrary", "arbitrary"), vmem_limit_bytes=VMEM_LIMIT_BYTES),
        name="dsw_out_proj",
    )(og, x, mod, w_out, final_g.reshape(1, D_MODEL))


def kernel(x, c, ada_w, ada_b, norm_g, gla_w_in, gla_w_alpha, gla_b_alpha, gla_norm_g, gla_w_out,
           dsw_w_in, dsw_w_out, rel_bias, final_g):
    batch, seq, d_model = x.shape
    assert d_model == D_MODEL and x.dtype == F32 and ada_w.shape[0] == DEPTH
    assert seq % GLA_ROW_TILE == 0 and seq % OUT_ROW_TILE == 0 and seq % (DSW_SPAN * DSW_GROUPS[-1][1]) == 0
    assert (batch * seq) % PROJ_ROW_TILE == 0 and batch % ATT_BATCH_PER_STEP == 0
    mod = _adaln_mod(c, ada_w, ada_b)
    mod = mod.reshape(DEPTH, batch, 3, D_MODEL)
    x, *h_orders = _gla_layer(x, mod[0], norm_g[0], gla_w_in[0], gla_w_alpha[0], gla_b_alpha[0],
                              gla_norm_g[0], gla_w_out[0], mod[1], norm_g[1])
    projs = []
    for gi in range(N_GROUPS):
        sections = [3 * gi, 3 * gi + 1, 3 * gi + 2]
        if gi == 0:
            sections.append(3 * N_GROUPS)
        projs.append(_dsw_group_proj(h_orders[gi].reshape(-1, D_MODEL), dsw_w_in, sections))
    bias = _dsw_bias(rel_bias)
    og = _dsw_attention(projs, bias, batch)
    return _dsw_out_proj(og, x, mod[1], dsw_w_out[0], final_g)
```

```python
import functools
import math

import jax
import jax.numpy as jnp
import numpy as np
from jax import lax
from jax.experimental import pallas as pl
from jax.experimental.pallas import tpu as pltpu

F32 = jnp.float32
BF16 = jnp.bfloat16

D_MODEL = 1024
DEPTH = 2
NORM_EPS = 1e-6

GLA_HEADS = 4
GLA_DK = D_MODEL // 2
GLA_DV = D_MODEL
GLA_DK_HEAD = GLA_DK // GLA_HEADS
GLA_DV_HEAD = GLA_DV // GLA_HEADS
GLA_RANK = 16
GLA_TAU = 16.0
GLA_CHUNK = 64

DSW_HEADS = 16
DSW_HEAD_DIM = D_MODEL // DSW_HEADS
DSW_WIDTH = DSW_HEADS * DSW_HEAD_DIM
DSW_GROUPS = ((128, 1), (512, 4), (2048, 16))
N_GROUPS = len(DSW_GROUPS)
DSW_SPAN = 128
REGATHER_STRIDE = 4
assert N_GROUPS == 3 and all(b[1] == a[1] * REGATHER_STRIDE for a, b in zip(DSW_GROUPS, DSW_GROUPS[1:]))


def _segment_residues():
    table = [[0]]
    for gi in range(1, N_GROUPS):
        prev_dil = DSW_GROUPS[gi - 1][1]
        table.append([res + prev_dil * r2 for res in table[-1] for r2 in range(REGATHER_STRIDE)])
    return table


SEG_RESIDUE = _segment_residues()
REL_BUCKETS = 32
REL_MAX_DIST = 2048

LANES = 128
HEADS_PER_STEP = LANES // DSW_HEAD_DIM
assert HEADS_PER_STEP == 2
ATT_BATCH_PER_STEP = 2
MERGE_GROUP = 0
assert DSW_GROUPS[MERGE_GROUP][1] == 1
LOG2E = math.log2(math.e)
QUERY_SCALE = DSW_HEAD_DIM ** -0.5 * LOG2E
VMEM_LIMIT_BYTES = 56 * 1024 * 1024

GLA_ROW_TILE = 512
BF16_SUBLANES = 16
assert GLA_ROW_TILE % GLA_CHUNK == 0 and (GLA_ROW_TILE // DSW_GROUPS[-1][1]) % BF16_SUBLANES == 0
OUT_ROW_TILE = 2048
PROJ_ROW_CHUNK = 512
PROJ_ROW_TILE = 4096

_NT = (((1,), (1,)), ((), ()))


def _dot(a, b, precision=None):
    return jnp.dot(a, b, preferred_element_type=F32, precision=precision)


def _dot_nt(a, b):
    return lax.dot_general(a, b, _NT, preferred_element_type=F32)


def _silu(v):
    return v * (1.0 / (1.0 + jnp.exp(-v)))


def _log_sigmoid(z):
    return jnp.minimum(z, 0.0) - jnp.log(1.0 + jnp.exp(-jnp.abs(z)))


def _chunk_cumsum(x, chunk):
    pos = lax.broadcasted_iota(jnp.int32, x.shape, 0) % chunk
    step = 1
    while step < chunk:
        x = x + jnp.where(pos >= step, pltpu.roll(x, shift=step, axis=0), 0.0)
        step *= 2
    return x


def _modulated_rmsnorm(x, g, shift, scale):
    y = x * lax.rsqrt(jnp.mean(x * x, axis=-1, keepdims=True) + NORM_EPS)
    return y * (g * (1.0 + scale)) + shift


def _adaln_kernel(c_ref, w_ref, b_ref, out_ref):
    c_act = _silu(c_ref[...])
    out_ref[...] = _dot(c_act.astype(BF16), w_ref[...].astype(BF16)) + b_ref[...]


def _adaln_mod(c, ada_w, ada_b):
    batch = c.shape[0]
    n_col = 3
    return pl.pallas_call(
        _adaln_kernel,
        grid=(DEPTH, n_col),
        in_specs=[
            pl.BlockSpec((batch, D_MODEL), lambda i, j: (0, 0)),
            pl.BlockSpec((None, D_MODEL, D_MODEL), lambda i, j: (i, 0, j)),
            pl.BlockSpec((None, 1, D_MODEL), lambda i, j: (i, 0, j)),
        ],
        out_specs=pl.BlockSpec((None, batch, D_MODEL), lambda i, j: (i, 0, j)),
        out_shape=jax.ShapeDtypeStruct((DEPTH, batch, 3 * D_MODEL), F32),
        compiler_params=pltpu.CompilerParams(
            dimension_semantics=("arbitrary", "arbitrary"), vmem_limit_bytes=VMEM_LIMIT_BYTES),
        name="adaln_mod",
    )(c, ada_w, ada_b.reshape(DEPTH, 1, 3 * D_MODEL))


def _gla_kernel(x_ref, mod_ref, ng_ref, wq_ref, wk_ref, wv_ref, wr_ref, wg_ref, wa_ref, ba_ref,
                hng_ref, wo_ref, next_mod_ref, next_ng_ref, out_ref, *rest):
    h_refs = rest[:N_GROUPS]
    state_ref, q_scr, k_scr, v_scr, b_scr, o_scr, hb_scr, hf_scr, tmp_scr = rest[N_GROUPS:]
    ts = GLA_ROW_TILE
    ch = GLA_CHUNK

    @pl.when(pl.program_id(1) == 0)
    def _():
        state_ref[...] = jnp.zeros_like(state_ref)

    shift, scale, gate = mod_ref[0:1, :], mod_ref[1:2, :], mod_ref[2:3, :]
    hb_scr[...] = _modulated_rmsnorm(x_ref[...], ng_ref[...], shift, scale).astype(BF16)
    g_lr = _dot(hb_scr[...], wg_ref[...])
    z = _dot(g_lr.astype(BF16), wa_ref[...]) + ba_ref[...]
    b_scr[...] = _chunk_cumsum(_log_sigmoid(z) / GLA_TAU, ch)
    q_scr[...] = _dot(hb_scr[...], wq_ref[...]) * (GLA_DK_HEAD ** -0.5)
    k_scr[...] = _dot(hb_scr[...], wk_ref[...])
    v_scr[...] = _dot(hb_scr[...], wv_ref[...])

    ci = lax.broadcasted_iota(jnp.int32, (ch, ch), 0)
    cj = lax.broadcasted_iota(jnp.int32, (ch, ch), 1)
    causal = cj <= ci
    n_chunks = ts // ch
    heads = range(GLA_HEADS)

    def k_lanes(hd):
        return pl.ds(hd * GLA_DK_HEAD, GLA_DK_HEAD)

    def v_lanes(hd):
        return pl.ds(hd * GLA_DV_HEAD, GLA_DV_HEAD)

    def intra_scores(c):
        rows = pl.ds(c * ch, ch)
        out = []
        for hd in heads:
            bc = b_scr[rows, k_lanes(hd)]
            mid = bc[ch // 2:ch // 2 + 1, :]
            qe = (q_scr[rows, k_lanes(hd)] * jnp.exp(bc - mid)).astype(BF16)
            ke = (k_scr[rows, k_lanes(hd)] * jnp.exp(mid - bc)).astype(BF16)
            out.append(_dot_nt(qe, ke))
        return out

    a_next = intra_scores(0)
    for c in range(n_chunks):
        rows = pl.ds(c * ch, ch)
        a_cur = a_next
        if c + 1 < n_chunks:
            a_next = intra_scores(c + 1)
        for hd in heads:
            bc = b_scr[rows, k_lanes(hd)]
            last = bc[ch - 1:ch, :]
            qc = q_scr[rows, k_lanes(hd)]
            kc = k_scr[rows, k_lanes(hd)]
            vcb = v_scr[rows, v_lanes(hd)].astype(BF16)
            st = state_ref[hd]
            kdt = (kc * jnp.exp(last - bc)).T.astype(BF16)
            a = jnp.where(causal, a_cur[hd], 0.0).astype(BF16)
            res = _dot(jnp.concatenate([a, kdt], axis=0), vcb)
            inter = _dot((qc * jnp.exp(bc)).astype(BF16), st.astype(BF16))
            decay = jnp.broadcast_to(jnp.exp(last), (GLA_DK_HEAD, GLA_DK_HEAD)).T
            decay = jnp.concatenate([decay] * (GLA_DV_HEAD // GLA_DK_HEAD), axis=1)
            state_ref[hd] = st * decay + res[ch:]
            o_scr[rows, v_lanes(hd)] = res[:ch] + inter

    r = _dot(hb_scr[...], wr_ref[...])
    parts = []
    for hd in range(GLA_HEADS):
        oh = o_scr[:, pl.ds(hd * GLA_DV_HEAD, GLA_DV_HEAD)]
        yh = oh * lax.rsqrt(jnp.mean(oh * oh, axis=-1, keepdims=True) + NORM_EPS)
        parts.append(yh * hng_ref[...])
    og = (jnp.concatenate(parts, axis=-1) * _silu(r)).astype(BF16)
    x_new = x_ref[...] + gate * _dot(og, wo_ref[...])
    out_ref[...] = x_new
    h_next = _modulated_rmsnorm(x_new, next_ng_ref[...], next_mod_ref[0:1, :], next_mod_ref[1:2, :])
    _store_group_orders(h_next, h_refs, hf_scr, tmp_scr)


def _gla_layer(x, mod, norm_g, w_in, w_alpha, b_alpha, head_norm_g, w_out, next_mod, next_norm_g):
    batch, seq, _ = x.shape
    ts = GLA_ROW_TILE
    o_q, o_k, o_v, o_g, o_r = 0, GLA_DK, 2 * GLA_DK, 2 * GLA_DK + GLA_DV, 2 * GLA_DK + GLA_DV + GLA_RANK
    w_in_b = w_in.astype(BF16)
    wq, wk, wv = w_in_b[:, o_q:o_k], w_in_b[:, o_k:o_v], w_in_b[:, o_v:o_g]
    wr = w_in_b[:, o_r:]
    wg = jnp.pad(w_in_b[:, o_g:o_r], ((0, 0), (0, LANES - GLA_RANK)))
    wa = jnp.pad(w_alpha, ((0, LANES - GLA_RANK), (0, 0))).astype(BF16)

    def const(shape):
        return pl.BlockSpec(shape, lambda b, t: (0,) * len(shape))

    return pl.pallas_call(
        _gla_kernel,
        grid=(batch, seq // ts),
        in_specs=[
            pl.BlockSpec((None, ts, D_MODEL), lambda b, t: (b, t, 0)),
            pl.BlockSpec((None, 3, D_MODEL), lambda b, t: (b, 0, 0)),
            const((1, D_MODEL)),
            const((D_MODEL, GLA_DK)), const((D_MODEL, GLA_DK)), const((D_MODEL, GLA_DV)),
            const((D_MODEL, GLA_DV)), const((D_MODEL, LANES)),
            const((LANES, GLA_DK)), const((1, GLA_DK)),
            const((1, GLA_DV_HEAD)), const((GLA_DV, D_MODEL)),
            pl.BlockSpec((None, 3, D_MODEL), lambda b, t: (b, 0, 0)),
            const((1, D_MODEL)),
        ],
        out_specs=[pl.BlockSpec((None, ts, D_MODEL), lambda b, t: (b, t, 0))] + [
            pl.BlockSpec((None, dil, ts // dil, D_MODEL), lambda b, t: (b, 0, t, 0)) for _, dil in DSW_GROUPS],
        out_shape=[jax.ShapeDtypeStruct(x.shape, F32)] + [
            jax.ShapeDtypeStruct((batch, dil, seq // dil, D_MODEL), BF16) for _, dil in DSW_GROUPS],
        scratch_shapes=[
            pltpu.VMEM((GLA_HEADS, GLA_DK_HEAD, GLA_DV_HEAD), F32),
            pltpu.VMEM((ts, GLA_DK), F32), pltpu.VMEM((ts, GLA_DK), F32),
            pltpu.VMEM((ts, GLA_DV), F32), pltpu.VMEM((ts, GLA_DK), F32),
            pltpu.VMEM((ts, GLA_DV), F32),
            pltpu.VMEM((ts, D_MODEL), BF16),
            pltpu.VMEM((D_MODEL // LANES, ts, LANES), F32),
            pltpu.VMEM((ts, LANES), F32),
        ],
        compiler_params=pltpu.CompilerParams(
            dimension_semantics=("arbitrary", "arbitrary"), vmem_limit_bytes=VMEM_LIMIT_BYTES),
        name="gla_layer",
    )(x, mod, norm_g.reshape(1, D_MODEL), wq, wk, wv, wr, wg, wa, b_alpha.reshape(1, GLA_DK),
      head_norm_g.reshape(1, GLA_DV_HEAD), w_out.astype(BF16), next_mod, next_norm_g.reshape(1, D_MODEL))


def _store_group_orders(h, out_refs, hf_scr, tmp_scr):
    n_rows = h.shape[0]
    out_refs[0][0] = h.astype(BF16)
    for cb in range(D_MODEL // LANES):
        hf_scr[cb] = h[:, cb * LANES:(cb + 1) * LANES]
    for cb in range(D_MODEL // LANES):
        lanes = pl.ds(cb * LANES, LANES)
        src = hf_scr.at[cb]
        for gi in range(1, N_GROUPS):
            prev_dil = DSW_GROUPS[gi - 1][1]
            prev_len = n_rows // prev_dil
            seg_len = prev_len // REGATHER_STRIDE
            for p in range(prev_dil):
                for r2 in range(REGATHER_STRIDE):
                    seg = p * REGATHER_STRIDE + r2
                    piece = src[pl.ds(p * prev_len + r2, seg_len, stride=REGATHER_STRIDE), :]
                    out_refs[gi][seg, :, lanes] = piece.astype(BF16)
                    if gi + 1 < N_GROUPS:
                        tmp_scr[pl.ds(seg * seg_len, seg_len), :] = piece
            src = tmp_scr


def _dsw_proj_kernel(h_ref, w_ref, out_ref):
    out_scale = jnp.where(pl.program_id(1) == 0, QUERY_SCALE, 1.0).astype(F32)
    w = w_ref[...].astype(BF16)
    for s in range(h_ref.shape[0] // PROJ_ROW_CHUNK):
        rows = pl.ds(s * PROJ_ROW_CHUNK, PROJ_ROW_CHUNK)
        out_ref[rows, :] = (_dot(h_ref[rows, :], w) * out_scale).astype(BF16)


def _dsw_group_proj(h, w_in, sections):
    m_rows = h.shape[0]
    n_sec = len(sections)
    tm = PROJ_ROW_TILE

    def w_block(i, j):
        col = functools.reduce(lambda a, b: a + b, [jnp.where(j == k, sec, 0) for k, sec in enumerate(sections)])
        return (0, 0, col)

    return pl.pallas_call(
        _dsw_proj_kernel,
        grid=(m_rows // tm, n_sec),
        in_specs=[
            pl.BlockSpec((tm, D_MODEL), lambda i, j: (i, 0)),
            pl.BlockSpec((None, D_MODEL, DSW_WIDTH), w_block),
        ],
        out_specs=pl.BlockSpec((None, tm, DSW_WIDTH), lambda i, j: (j, i, 0)),
        out_shape=jax.ShapeDtypeStruct((n_sec, m_rows, DSW_WIDTH), BF16),
        compiler_params=pltpu.CompilerParams(
            dimension_semantics=("arbitrary", "arbitrary"), vmem_limit_bytes=VMEM_LIMIT_BYTES),
        name="dsw_group_proj",
    )(h, w_in)


def _t5_causal_bucket(n):
    max_exact = REL_BUCKETS // 2
    nf = np.maximum(n, 1).astype(np.float32)
    large = max_exact + (np.log(nf / max_exact) / math.log(REL_MAX_DIST / max_exact)
                         * (REL_BUCKETS - max_exact)).astype(np.int32)
    large = np.minimum(large, REL_BUCKETS - 1)
    return np.where(n < max_exact, n, large).astype(np.int32)


def _bucket_tables():
    qi = np.arange(DSW_SPAN)[:, None]
    kj = np.arange(2 * DSW_SPAN)[None, :]
    steps = qi + DSW_SPAN - kj
    in_window = (steps >= 0) & (steps <= DSW_SPAN)
    tables = []
    for _, dil in DSW_GROUPS:
        bucket = _t5_causal_bucket(np.clip(steps, 0, DSW_SPAN) * dil)
        tables.append(np.where(in_window, bucket, -1))
    return np.stack(tables).astype(np.int32)


def _dsw_bias_kernel(rel_ref, bucket_ref, out_ref):
    bucket = bucket_ref[...]
    masks = [bucket == bk for bk in range(REL_BUCKETS)]
    outside = jnp.where(bucket < 0, -jnp.inf, 0.0).astype(F32)
    for h in range(DSW_HEADS):
        row = outside
        for bk in range(REL_BUCKETS):
            row = jnp.where(masks[bk], rel_ref[bk, h] * LOG2E, row)
        tile = jnp.broadcast_to(row, (DSW_SPAN, 2 * DSW_SPAN))
        out_ref[pl.ds(h * DSW_SPAN, DSW_SPAN), :] = pltpu.roll(tile, shift=0, axis=1, stride=1, stride_axis=0)


def _dsw_bias(rel_bias):
    return pl.pallas_call(
        _dsw_bias_kernel,
        grid=(N_GROUPS,),
        in_specs=[
            pl.BlockSpec(memory_space=pltpu.SMEM),
            pl.BlockSpec((None, 1, 2 * DSW_SPAN), lambda g: (g, 0, 0)),
        ],
        out_specs=pl.BlockSpec((None, DSW_HEADS * DSW_SPAN, 2 * DSW_SPAN), lambda g: (g, 0, 0)),
        out_shape=jax.ShapeDtypeStruct((N_GROUPS, DSW_HEADS * DSW_SPAN, 2 * DSW_SPAN), F32),
        compiler_params=pltpu.CompilerParams(dimension_semantics=("arbitrary",)),
        name="dsw_bias",
    )(rel_bias, jnp.asarray(_bucket_tables()[:, :1, :]))


def _dsw_attention_kernel(*refs):
    qkv_refs = refs[:3 * N_GROUPS]
    gate_ref, bias_ref, out_ref, pv_scr, m_scr, l_scr = refs[3 * N_GROUPS:]
    n_batch, seq, _ = out_ref.shape
    span = DSW_SPAN
    lane = lax.broadcasted_iota(jnp.int32, (span, LANES), 1)
    head_masks = [(lane // DSW_HEAD_DIM) == hh for hh in range(HEADS_PER_STEP)]

    def is_first(gi, blk):
        return blk % (seq // DSW_GROUPS[gi][1] // span) == 0

    def key_rows(gi, blk):
        return pl.ds(blk * span, span) if is_first(gi, blk) else pl.ds((blk - 1) * span, 2 * span)

    def attend():
        def scores(bb, gi, blk):
            q = qkv_refs[3 * gi][bb, pl.ds(blk * span, span), :]
            q2 = jnp.concatenate([jnp.where(mk, q, jnp.zeros_like(q)) for mk in head_masks], axis=0)
            return _dot_nt(q2, qkv_refs[3 * gi + 1][bb, key_rows(gi, blk), :])

        def biased_max(bb, gi, blk, s):
            if is_first(gi, blk):
                s = s + bias_ref[gi, :, pl.ds(span, span)]
            else:
                s = s + bias_ref[gi]
            return s, jnp.max(s, axis=-1, keepdims=True)

        units = [(bb, gi, blk) for bb in range(n_batch) for gi in reversed(range(N_GROUPS))
                 for blk in range(seq // span)]
        raw = {i: scores(*units[i]) for i in range(2)}
        staged = {0: biased_max(*units[0], raw.pop(0))}
        for idx, (bb, gi, blk) in enumerate(units):
            dil = DSW_GROUPS[gi][1]
            if idx + 2 < len(units):
                raw[idx + 2] = scores(*units[idx + 2])
            if idx + 1 < len(units):
                staged[idx + 1] = biased_max(*units[idx + 1], raw.pop(idx + 1))
            s, m = staged.pop(idx)
            e = jnp.exp2(s - m)
            l = jnp.sum(e, axis=-1, keepdims=True)
            pv = _dot(e.astype(BF16), qkv_refs[3 * gi + 2][bb, key_rows(gi, blk), :])
            blocks_per_seg = seq // dil // span
            residue = SEG_RESIDUE[gi][blk // blocks_per_seg]
            i0 = (blk % blocks_per_seg) * span
            dst = pl.ds(i0 * dil + residue, span, stride=dil) if dil > 1 else pl.ds(blk * span, span)
            stats = [jnp.where(head_masks[0], a[:span], a[span:]) for a in (pv, m, l)]
            if dil > REGATHER_STRIDE:
                coarse = dil // REGATHER_STRIDE
                quarter = seq // REGATHER_STRIDE
                staged_dst = pl.ds((residue % REGATHER_STRIDE) * quarter + residue // REGATHER_STRIDE + coarse * i0,
                                   span, stride=coarse)
                for scr, a in zip((pv_scr, m_scr, l_scr), stats):
                    scr[N_GROUPS - 1, staged_dst, :] = a
                if blk == seq // span - 1:
                    for scr in (pv_scr, m_scr, l_scr):
                        for r1 in range(REGATHER_STRIDE):
                            scr[gi - 1, pl.ds(r1, quarter, stride=REGATHER_STRIDE), :] = (
                                scr[N_GROUPS - 1, pl.ds(r1 * quarter, quarter), :])
                continue
            if gi != MERGE_GROUP:
                for scr, a in zip((pv_scr, m_scr, l_scr), stats):
                    scr[gi - 1, dst, :] = a
                continue
            pvs, tops, sums = [[stats[i]] + [scr[g - 1, dst, :] for g in range(1, N_GROUPS)]
                               for i, scr in enumerate((pv_scr, m_scr, l_scr))]
            top = functools.reduce(jnp.maximum, tops)
            wts = [jnp.exp2(v - top) for v in tops]
            num = functools.reduce(lambda a, b: a + b, [w * a for w, a in zip(wts, pvs)])
            den = functools.reduce(lambda a, b: a + b, [w * a for w, a in zip(wts, sums)])
            out_ref[bb, dst, :] = (num * _silu(gate_ref[bb, dst, :].astype(F32)) / den).astype(BF16)

    attend()


def _dsw_attention(projs, bias, batch):
    seq = projs[0].shape[1] // batch
    n_pairs = DSW_HEADS // HEADS_PER_STEP
    bps = ATT_BATCH_PER_STEP
    operands, in_specs = [], []

    def add(arr, j):
        operands.append(arr.reshape(arr.shape[0], batch, seq, DSW_WIDTH))
        in_specs.append(pl.BlockSpec((None, bps, seq, LANES), lambda b, hp, j=j: (j, b, 0, hp)))

    for gi in range(N_GROUPS):
        for j in range(3):
            add(projs[gi], j)
    add(projs[0], 3)
    in_specs.append(
        pl.BlockSpec((N_GROUPS, HEADS_PER_STEP * DSW_SPAN, 2 * DSW_SPAN), lambda b, hp: (0, hp, 0)))
    return pl.pallas_call(
        _dsw_attention_kernel,
        grid=(batch // bps, n_pairs),
        in_specs=in_specs,
        out_specs=pl.BlockSpec((bps, seq, LANES), lambda b, hp: (b, 0, hp)),
        out_shape=jax.ShapeDtypeStruct((batch, seq, DSW_WIDTH), BF16),
        scratch_shapes=[pltpu.VMEM((N_GROUPS, seq, LANES), F32)] * 3,
        compiler_params=pltpu.CompilerParams(
            dimension_semantics=("arbitrary", "arbitrary"), vmem_limit_bytes=VMEM_LIMIT_BYTES),
        name="dsw_attention",
    )(*operands, bias)


def _dsw_out_proj_kernel(o_ref, x_ref, mod_ref, w_ref, fg_ref, out_ref):
    gate = mod_ref[2:3, :]
    w = w_ref[...].astype(BF16)
    for s in range(o_ref.shape[0] // PROJ_ROW_CHUNK):
        rows = pl.ds(s * PROJ_ROW_CHUNK, PROJ_ROW_CHUNK)
        xn = x_ref[rows, :] + gate * _dot(o_ref[rows, :], w)
        out_ref[rows, :] = xn * lax.rsqrt(jnp.mean(xn * xn, axis=-1, keepdims=True) + NORM_EPS) * fg_ref[...]


def _dsw_out_proj(og, x, mod, w_out, final_g):
    batch, seq, _ = x.shape
    ts = OUT_ROW_TILE
    return pl.pallas_call(
        _dsw_out_proj_kernel,
        grid=(batch, seq // ts),
        in_specs=[
            pl.BlockSpec((None, ts, DSW_WIDTH), lambda b, t: (b, t, 0)),
            pl.BlockSpec((None, ts, D_MODEL), lambda b, t: (b, t, 0)),
            pl.BlockSpec((None, 3, D_MODEL), lambda b, t: (b, 0, 0)),
            pl.BlockSpec((DSW_WIDTH, D_MODEL), lambda b, t: (0, 0)),
            pl.BlockSpec((1, D_MODEL), lambda b, t: (0, 0)),
        ],
        out_specs=pl.BlockSpec((None, ts, D_MODEL), lambda b, t: (b, t, 0)),
        out_shape=jax.ShapeDtypeStruct(x.shape, F32),
        compiler_params=pltpu.CompilerParams(
            dimension_semantics=("arbitrary", "arbitrary"), vmem_limit_bytes=VMEM_LIMIT_BYTES),
        name="dsw_out_proj",
    )(og, x, mod, w_out, final_g.reshape(1, D_MODEL))


def kernel(x, c, ada_w, ada_b, norm_g, gla_w_in, gla_w_alpha, gla_b_alpha, gla_norm_g, gla_w_out,
           dsw_w_in, dsw_w_out, rel_bias, final_g):
    batch, seq, d_model = x.shape
    assert d_model == D_MODEL and x.dtype == F32 and ada_w.shape[0] == DEPTH
    assert seq % GLA_ROW_TILE == 0 and seq % OUT_ROW_TILE == 0 and seq % (DSW_SPAN * DSW_GROUPS[-1][1]) == 0
    assert (batch * seq) % PROJ_ROW_TILE == 0 and batch % ATT_BATCH_PER_STEP == 0
    mod = _adaln_mod(c, ada_w, ada_b)
    mod = mod.reshape(DEPTH, batch, 3, D_MODEL)
    x, *h_orders = _gla_layer(x, mod[0], norm_g[0], gla_w_in[0], gla_w_alpha[0], gla_b_alpha[0],
                              gla_norm_g[0], gla_w_out[0], mod[1], norm_g[1])
    projs = []
    for gi in range(N_GROUPS):
        sections = [3 * gi, 3 * gi + 1, 3 * gi + 2]
        if gi == 0:
            sections.append(3 * N_GROUPS)
        projs.append(_dsw_group_proj(h_orders[gi].reshape(-1, D_MODEL), dsw_w_in, sections))
    bias = _dsw_bias(rel_bias)
    og = _dsw_attention(projs, bias, batch)
    return _dsw_out_proj(og, x, mod[1], dsw_w_out[0], final_g)
```
